```python
import jax, jax.numpy as jnp
from jax import lax
import numpy as np

D_MODEL = 2048
BATCH = 2
SEQ = 4096
DEPTH = 2

HEAD_DIM = 128
A_Q_HEADS = 8
A_KV_HEADS = 2
B_GROUPS = ((128, 1), (512, 4), (2048, 16))
B_HEADS_PER_GROUP = 2
B_HEADS = B_HEADS_PER_GROUP * len(B_GROUPS)
B_Q_BLOCK = 64
C_Q_HEADS = 8
C_KV_HEADS = 2
C_HALF_WINDOW = 128
Q_BLOCK = 128
N_BRANCHES = 3
GRID_W = 64
ROPE_THETA = 10000.0
PEER_HEADS = 8
PEER_N_KEYS = 128
PEER_N_EXPERTS = PEER_N_KEYS * PEER_N_KEYS
PEER_QUERY_DIM = 256
PEER_TOPK = 16
PEER_TOKEN_CHUNK = 128
LN_EPS = 1e-5
QK_EPS = 1e-6
NEG_INF = -1e30
DEEPNORM_ALPHA = (2 * DEPTH) ** 0.25
DEEPNORM_BETA = (8 * DEPTH) ** -0.25

A_Q_W = A_Q_HEADS * HEAD_DIM
A_KV_W = A_KV_HEADS * HEAD_DIM
B_W = B_HEADS * HEAD_DIM
C_Q_W = C_Q_HEADS * HEAD_DIM
C_KV_W = C_KV_HEADS * HEAD_DIM
IN_SPLITS = (A_Q_W, A_KV_W, A_KV_W, B_W, B_W, B_W, C_Q_W, C_KV_W, C_KV_W, N_BRANCHES * D_MODEL)
IN_WIDTH = sum(IN_SPLITS)

kernel_name = "hybrid_gated_mixers_peer_deepnorm_encoder"


def layer_norm(x, g, b):
    xf = x.astype(jnp.float32)
    mu = jnp.mean(xf, axis=-1, keepdims=True)
    var = jnp.mean(jnp.square(xf - mu), axis=-1, keepdims=True)
    return ((xf - mu) * lax.rsqrt(var + LN_EPS) * g + b).astype(x.dtype)


def rms_norm(x, g):
    xf = x.astype(jnp.float32)
    return (xf * lax.rsqrt(jnp.mean(jnp.square(xf), axis=-1, keepdims=True) + QK_EPS) * g).astype(x.dtype)


def alibi_slopes(n):
    return jnp.asarray(2.0 ** (-8.0 * np.arange(1, n + 1) / n), dtype=jnp.float32)


def axial_rope_tables(seq):
    rows = seq // GRID_W
    row = jnp.repeat(jnp.arange(rows, dtype=jnp.float32), GRID_W)
    col = jnp.tile(jnp.arange(GRID_W, dtype=jnp.float32), rows)
    quarter = HEAD_DIM // 4
    inv = ROPE_THETA ** (-jnp.arange(quarter, dtype=jnp.float32) / quarter)
    ang_r = row[:, None] * inv
    ang_c = col[:, None] * inv
    return jnp.cos(ang_r), jnp.sin(ang_r), jnp.cos(ang_c), jnp.sin(ang_c)


def axial_rope(x, cos_r, sin_r, cos_c, sin_c):
    xf = x.astype(jnp.float32)
    bshape = (1, x.shape[1]) + (1,) * (x.ndim - 3) + (-1,)

    def rot(xh, cos, sin):
        cos = cos.reshape(bshape)
        sin = sin.reshape(bshape)
        x1, x2 = jnp.split(xh, 2, axis=-1)
        return jnp.concatenate([x1 * cos - x2 * sin, x2 * cos + x1 * sin], axis=-1)

    xr, xc = jnp.split(xf, 2, axis=-1)
    return jnp.concatenate([rot(xr, cos_r, sin_r), rot(xc, cos_c, sin_c)], axis=-1).astype(x.dtype)


def banded_attention(q, k, v, half_win, q_block, slopes, dist_scale, sink=None):
    n, seq_len, hkv, grp, hd = q.shape
    nb = -(-seq_len // q_block)
    lp = nb * q_block
    q = jnp.pad(q, ((0, 0), (0, lp - seq_len), (0, 0), (0, 0), (0, 0)))
    pad_k = ((0, 0), (half_win, half_win + lp - seq_len), (0, 0), (0, 0))
    kp = jnp.pad(k, pad_k)
    vp = jnp.pad(v, pad_k)
    span = q_block + 2 * half_win
    starts = jnp.arange(nb) * q_block
    idx = starts[:, None] + jnp.arange(span)[None]
    kpos = idx - half_win
    qpos = starts[:, None] + jnp.arange(q_block)[None]
    dist = jnp.abs(qpos[:, :, None] - kpos[:, None, :])
    valid = (dist <= half_win) & (kpos[:, None, :] >= 0) & (kpos[:, None, :] < seq_len)
    kb = kp[:, idx]
    vb = vp[:, idx]
    qb = q.reshape(n, nb, q_block, hkv, grp, hd)
    s = jnp.einsum('nbqhgd,nbkhd->nbhgqk', qb, kb).astype(jnp.float32) * (hd ** -0.5)
    bias = slopes.astype(jnp.float32)[None, :, :, None, None] * (dist * dist_scale).astype(jnp.float32)[:, None, None]
    s = jnp.where(valid[:, None, None], s - bias, NEG_INF)
    m = jnp.max(s, axis=-1)
    if sink is not None:
        sk = sink.astype(jnp.float32)[None, None, :, :, None]
        m = jnp.maximum(m, sk)
    p = jnp.exp(s - m[..., None])
    denom = jnp.sum(p, axis=-1)
    if sink is not None:
        denom = denom + jnp.exp(sk - m)
    o = jnp.einsum('nbhgqk,nbkhd->nbqhgd', p.astype(v.dtype), vb).astype(jnp.float32)
    o = (o / jnp.moveaxis(denom, -1, 2)[..., None]).astype(v.dtype)
    lse = jnp.moveaxis(m + jnp.log(denom), -1, 2)
    o = o.reshape(n, lp, hkv, grp, hd)[:, :seq_len]
    lse = lse.reshape(n, lp, hkv, grp)[:, :seq_len]
    return o, lse


def mixer_a(q, k, v, q_gain, k_gain):
    b, s, _ = q.shape
    grp = A_Q_HEADS // A_KV_HEADS
    q = rms_norm(q.reshape(b, s, A_KV_HEADS, grp, HEAD_DIM), q_gain)
    k = rms_norm(k.reshape(b, s, A_KV_HEADS, HEAD_DIM), k_gain)
    v = v.reshape(b, s, A_KV_HEADS, HEAD_DIM)
    tables = axial_rope_tables(s)
    q = axial_rope(q, *tables)
    k = axial_rope(k, *tables)
    nb = s // Q_BLOCK
    qb = jnp.moveaxis(q.reshape(b, nb, Q_BLOCK, A_KV_HEADS, grp, HEAD_DIM), 1, 0)

    def one_block(qblk):
        sc = jnp.einsum('bqhgd,bkhd->bhgqk', qblk, k).astype(jnp.float32) * (HEAD_DIM ** -0.5)
        p = jax.nn.softmax(sc, axis=-1).astype(v.dtype)
        return jnp.einsum('bhgqk,bkhd->bqhgd', p, v)

    o = lax.map(one_block, qb)
    return jnp.moveaxis(o, 0, 1).reshape(b, s, A_Q_W)


def mixer_b(q, k, v):
    b, s, _ = q.shape
    hg = B_HEADS_PER_GROUP
    q = q.reshape(b, s, B_HEADS, HEAD_DIM)
    k = k.reshape(b, s, B_HEADS, HEAD_DIM)
    v = v.reshape(b, s, B_HEADS, HEAD_DIM)
    slopes = alibi_slopes(B_HEADS)
    outs, lses = [], []
    for g, (win, r) in enumerate(B_GROUPS):
        lo, hi = g * hg, (g + 1) * hg
        sub = s // r

        def strided(t):
            return t[:, :, lo:hi].reshape(b, sub, r, hg, HEAD_DIM).transpose(0, 2, 1, 3, 4).reshape(b * r, sub, hg, HEAD_DIM)

        o, lse = banded_attention(strided(q)[:, :, :, None], strided(k), strided(v),
                                  (win // 2) // r, B_Q_BLOCK, slopes[lo:hi][:, None], r)
        outs.append(o.reshape(b, r, sub, hg, HEAD_DIM).transpose(0, 2, 1, 3, 4).reshape(b, s, hg, HEAD_DIM))
        lses.append(lse.reshape(b, r, sub, hg).transpose(0, 2, 1, 3).reshape(b, s, hg))
    o = jnp.stack(outs, axis=2)
    w = jax.nn.softmax(jnp.stack(lses, axis=2), axis=2)
    return (o * w[..., None].astype(o.dtype)).reshape(b, s, B_W)


def mixer_c(q, k, v, sink):
    b, s, _ = q.shape
    grp = C_Q_HEADS // C_KV_HEADS
    q = q.reshape(b, s, C_KV_HEADS, grp, HEAD_DIM)
    k = k.reshape(b, s, C_KV_HEADS, HEAD_DIM)
    v = v.reshape(b, s, C_KV_HEADS, HEAD_DIM)
    slopes = alibi_slopes(C_Q_HEADS).reshape(C_KV_HEADS, grp)
    o, _ = banded_attention(q, k, v, C_HALF_WINDOW, Q_BLOCK, slopes, 1, sink)
    return o.reshape(b, s, C_Q_W)


def peer(h, w_q, sub_keys, u, v):
    b, s, d = h.shape
    t = b * s
    xt = h.reshape(t, d)
    q = (xt @ w_q).reshape(t, PEER_HEADS, 2, PEER_QUERY_DIM // 2)
    sc = jnp.einsum('thpd,pkd->thpk', q, sub_keys).astype(jnp.float32)
    vals, idx = lax.top_k(sc, PEER_TOPK)
    cand = vals[:, :, 0, :, None] + vals[:, :, 1, None, :]
    cand_idx = idx[:, :, 0, :, None] * PEER_N_KEYS + idx[:, :, 1, None, :]
    top_v, pos = lax.top_k(cand.reshape(t, PEER_HEADS, PEER_TOPK * PEER_TOPK), PEER_TOPK)
    expert = jnp.take_along_axis(cand_idx.reshape(t, PEER_HEADS, -1), pos, axis=-1)
    gate = jax.nn.softmax(top_v, axis=-1).astype(h.dtype)
    expert = expert.reshape(t, PEER_HEADS * PEER_TOPK)
    gate = gate.reshape(t, PEER_HEADS * PEER_TOPK)
    nc = t // PEER_TOKEN_CHUNK

    def chunk(args):
        xc, ec, gc = args
        act = jax.nn.gelu(jnp.einsum('cd,ckd->ck', xc, u[ec]), approximate=False)
        return jnp.einsum('ck,ckd->cd', gc * act, v[ec])

    out = lax.map(chunk, (xt.reshape(nc, PEER_TOKEN_CHUNK, d),
                          expert.reshape(nc, PEER_TOKEN_CHUNK, -1),
                          gate.reshape(nc, PEER_TOKEN_CHUNK, -1)))
    return out.reshape(b, s, d)


def setup_inputs(seed: int = 0) -> dict:
    key = jax.random.key(seed)
    ks = jax.random.split(key, 24)
    L, D = DEPTH, D_MODEL

    def nrm(k, shape, scale):
        return jax.random.normal(k, shape, jnp.float32) * scale

    return {
        "x": nrm(ks[0], (BATCH, SEQ, D), 1.0),
        "c": nrm(ks[1], (BATCH, D), 1.0),
        "w_mod": nrm(ks[2], (L, D, 6 * D), 0.1 * D ** -0.5),
        "b_mod": nrm(ks[3], (L, 6 * D), 0.02),
        "w_in": nrm(ks[4], (L, D, IN_WIDTH), D ** -0.5),
        "a_q_gain": 1.0 + nrm(ks[5], (L, HEAD_DIM), 0.02),
        "a_k_gain": 1.0 + nrm(ks[6], (L, HEAD_DIM), 0.02),
        "c_sink": nrm(ks[7], (L, C_KV_HEADS, C_Q_HEADS // C_KV_HEADS), 0.5),
        "w_pa": nrm(ks[8], (L, A_Q_W, D), DEEPNORM_BETA * A_Q_W ** -0.5),
        "w_pb": nrm(ks[9], (L, B_W, D), DEEPNORM_BETA * B_W ** -0.5),
        "w_pc": nrm(ks[10], (L, C_Q_W, D), DEEPNORM_BETA * C_Q_W ** -0.5),
        "w_o": nrm(ks[11], (L, D, D), DEEPNORM_BETA * D ** -0.5),
        "ln1_g": 1.0 + nrm(ks[12], (L, D), 0.02),
        "ln1_b": nrm(ks[13], (L, D), 0.02),
        "peer_wq": nrm(ks[14], (L, D, PEER_HEADS * PEER_QUERY_DIM), D ** -0.5),
        "peer_keys": nrm(ks[15], (L, 2, PEER_N_KEYS, PEER_QUERY_DIM // 2), (PEER_QUERY_DIM // 2) ** -0.5),
        "peer_u": nrm(ks[16], (L, PEER_N_EXPERTS, D), D ** -0.5),
        "peer_v": nrm(ks[17], (L, PEER_N_EXPERTS, D), DEEPNORM_BETA * PEER_HEADS ** -0.5),
        "ln2_g": 1.0 + nrm(ks[18], (L, D), 0.02),
        "ln2_b": nrm(ks[19], (L, D), 0.02),
    }


def reference(x, c, w_mod, b_mod, w_in, a_q_gain, a_k_gain, c_sink, w_pa, w_pb, w_pc, w_o,
              ln1_g, ln1_b, peer_wq, peer_keys, peer_u, peer_v, ln2_g, ln2_b):
    b, s, d = x.shape
    split_at = np.cumsum(IN_SPLITS)[:-1].tolist()
    for l in range(DEPTH):
        mod = (c @ w_mod[l] + b_mod[l])[:, None, :]
        sh_a, sc_a, g_a, sh_f, sc_f, g_f = jnp.split(mod, 6, axis=-1)
        h = x * (1.0 + sc_a) + sh_a
        qa, ka, va, qb, kb, vb, qc, kc, vc, gl = jnp.split(h @ w_in[l], split_at, axis=-1)
        ya = mixer_a(qa, ka, va, a_q_gain[l], a_k_gain[l]) @ w_pa[l]
        yb = mixer_b(qb, kb, vb) @ w_pb[l]
        yc = mixer_c(qc, kc, vc, c_sink[l]) @ w_pc[l]
        gates = jax.nn.sigmoid(gl.reshape(b, s, N_BRANCHES, d))
        merged = gates[:, :, 0] * ya + gates[:, :, 1] * yb + gates[:, :, 2] * yc
        y = merged @ w_o[l]
        x = layer_norm(DEEPNORM_ALPHA * x + (1.0 + g_a) * y, ln1_g[l], ln1_b[l])
        h = x * (1.0 + sc_f) + sh_f
        y = peer(h, peer_wq[l], peer_keys[l], peer_u[l], peer_v[l])
        x = layer_norm(DEEPNORM_ALPHA * x + (1.0 + g_f) * y, ln2_g[l], ln2_b[l])
    return x
```

```python
import functools

import numpy as np
import jax
import jax.numpy as jnp
from jax import lax
from jax.experimental import pallas as pl
from jax.experimental.pallas import tpu as pltpu

F32 = jnp.float32
BF16 = jnp.bfloat16

HEAD_DIM = 128
A_Q_HEADS, A_KV_HEADS = 8, 2
B_GROUPS = ((128, 1), (512, 4), (2048, 16))
B_HEADS_PER_GROUP = 2
B_HEADS = B_HEADS_PER_GROUP * len(B_GROUPS)
C_Q_HEADS, C_KV_HEADS = 8, 2
C_HALF_WINDOW = 128
N_BRANCHES = 3
GRID_W = 64
ROPE_THETA = 10000.0
PEER_HEADS = 8
PEER_N_KEYS = 128
PEER_TOPK = 16
LN_EPS = 1e-5
QK_EPS = 1e-6
NEG_INF = -1e30

LANES = 128
VMEM_LIMIT = 56 * 1024 * 1024


def _cparams(*sem):
    return pltpu.CompilerParams(dimension_semantics=sem, vmem_limit_bytes=VMEM_LIMIT)


def _mod_kernel(cb_ref, w_ref, b_ref, o_ref):
    nb, tn = cb_ref.shape[0], w_ref.shape[2]
    rows = []
    for b in range(nb):
        cb = cb_ref[b]
        pieces = [jnp.sum(w_ref[0, :, n * LANES:(n + 1) * LANES] * cb, axis=0, keepdims=True)
                  for n in range(tn // LANES)]
        rows.append(jnp.concatenate(pieces, axis=1))
    o_ref[0] = jnp.concatenate(rows, axis=0) + b_ref[0]


def _modulation(c, w_mod, b_mod, tn=1024):
    nl, d, n6 = w_mod.shape
    nb = c.shape[0]
    cb = jnp.broadcast_to(c[:, :, None], (nb, d, LANES))
    return pl.pallas_call(
        _mod_kernel,
        grid=(nl, n6 // tn),
        in_specs=[pl.BlockSpec((nb, d, LANES), lambda l, j: (0, 0, 0)),
                  pl.BlockSpec((1, d, tn), lambda l, j: (l, 0, j)),
                  pl.BlockSpec((1, 1, tn), lambda l, j: (l, 0, j))],
        out_specs=pl.BlockSpec((1, nb, tn), lambda l, j: (l, 0, j)),
        out_shape=jax.ShapeDtypeStruct((nl, nb, n6), F32),
        compiler_params=_cparams("parallel", "parallel"),
        name="adaln_modulation",
    )(cb, w_mod, b_mod.reshape(nl, 1, n6))


IN_TN = 768
IN_TILES = (2, 3, 2, 8)


def _inproj_kernel(x_ref, sc_ref, sh_ref, w_ref, oa_ref, ob_ref, oc_ref, og_ref, h_scr):
    j = pl.program_id(1)

    @pl.when(j == 0)
    def _():
        h_scr[...] = (x_ref[...] * (1.0 + sc_ref[0]) + sh_ref[0]).astype(BF16)

    res = jnp.dot(h_scr[...], w_ref[...], preferred_element_type=F32).astype(BF16)
    lo = 0
    for ref, n in zip((oa_ref, ob_ref, oc_ref, og_ref), IN_TILES):
        @pl.when((j >= lo) & (j < lo + n))
        def _(ref=ref):
            ref[...] = res
        lo += n


def _in_projection(x2d, mod3, w_bf, seq, tm=1024):
    t, d = x2d.shape
    tm = min(tm, seq)
    per_b = seq // tm
    starts = np.cumsum((0,) + IN_TILES[:-1]).tolist()

    def out_spec(lo, n):
        return pl.BlockSpec((tm, IN_TN), lambda i, j: (i, jnp.clip(j - lo, 0, n - 1)))

    return pl.pallas_call(
        _inproj_kernel,
        grid=(t // tm, sum(IN_TILES)),
        in_specs=[pl.BlockSpec((tm, d), lambda i, j: (i, 0)),
                  pl.BlockSpec((1, 1, d), lambda i, j: (i // per_b, 0, 1)),
                  pl.BlockSpec((1, 1, d), lambda i, j: (i // per_b, 0, 0)),
                  pl.BlockSpec((d, IN_TN), lambda i, j: (0, j))],
        out_specs=[out_spec(lo, n) for lo, n in zip(starts, IN_TILES)],
        out_shape=[jax.ShapeDtypeStruct((t, n * IN_TN), BF16) for n in IN_TILES],
        scratch_shapes=[pltpu.VMEM((tm, d), BF16)],
        compiler_params=_cparams("parallel", "arbitrary"),
        name="in_projection",
    )(x2d, mod3, mod3, w_bf)


def _rope_tables(seq):
    rows = seq // GRID_W
    row = jnp.repeat(jnp.arange(rows, dtype=F32), GRID_W)
    col = jnp.tile(jnp.arange(GRID_W, dtype=F32), rows)
    quarter = HEAD_DIM // 4
    inv = ROPE_THETA ** (-jnp.arange(quarter, dtype=F32) / quarter)
    ar, ac = row[:, None] * inv, col[:, None] * inv
    ctab = jnp.concatenate([jnp.cos(ar), jnp.cos(ar), jnp.cos(ac), jnp.cos(ac)], axis=1)
    stab = jnp.concatenate([-jnp.sin(ar), jnp.sin(ar), -jnp.sin(ac), jnp.sin(ac)], axis=1)
    return ctab, stab


def _norm_rope(xf, gain, ctab, stab):
    ms = jnp.mean(xf * xf, axis=-1, keepdims=True)
    xn = xf * lax.rsqrt(ms + QK_EPS) * gain
    lane = lax.broadcasted_iota(jnp.int32, xn.shape, 1)
    partner = jnp.where((lane & 63) < 32, pltpu.roll(xn, 96, axis=1), pltpu.roll(xn, 32, axis=1))
    return xn * ctab + partner * stab


def _attn_a_kernel(q_ref, k_ref, v_ref, cq_ref, sq_ref, ck_ref, sk_ref, qg_ref, kg_ref,
                   o_ref, kp_scr, *, kc):
    tq, seq = q_ref.shape[0], k_ref.shape[0]
    grp = q_ref.shape[1] // HEAD_DIM

    @pl.when(pl.program_id(2) == 0)
    def _():
        kp_scr[...] = _norm_rope(k_ref[...].astype(F32), kg_ref[...], ck_ref[...], sk_ref[...]).astype(BF16)

    cq, sq, qg = cq_ref[...], sq_ref[...], qg_ref[...]
    scale = HEAD_DIM ** -0.5
    q4 = jnp.concatenate(
        [(_norm_rope(q_ref[:, h * HEAD_DIM:(h + 1) * HEAD_DIM].astype(F32), qg, cq, sq) * scale).astype(BF16)
         for h in range(grp)], axis=0)

    def body(c, carry):
        m, l, acc = carry
        k0 = pl.multiple_of(c * kc, kc)
        s = lax.dot_general(q4, kp_scr[pl.ds(k0, kc), :], (((1,), (1,)), ((), ())),
                            preferred_element_type=F32)
        mn = jnp.maximum(m, jnp.max(s, axis=-1, keepdims=True))
        a = jnp.exp(m - mn)
        p = jnp.exp(s - mn)
        l = a * l + jnp.sum(p, axis=-1, keepdims=True)
        acc = a * acc + jnp.dot(p.astype(BF16), v_ref[pl.ds(k0, kc), :], preferred_element_type=F32)
        return mn, l, acc

    rows = grp * tq
    m, l, acc = lax.fori_loop(0, seq // kc, body,
                              (jnp.full((rows, 1), -jnp.inf, F32), jnp.zeros((rows, 1), F32),
                               jnp.zeros((rows, HEAD_DIM), F32)))
    o = acc / l
    for h in range(grp):
        o_ref[:, h * HEAD_DIM:(h + 1) * HEAD_DIM] = o[h * tq:(h + 1) * tq].astype(BF16)


def _mixer_a(qkv, ctab, stab, q_gain, k_gain, nb, seq, tq=256, kc=512):
    t = qkv.shape[0]
    nq = seq // tq
    gw = (A_Q_HEADS // A_KV_HEADS) * HEAD_DIM
    k_blk = A_Q_HEADS
    v_blk = A_Q_HEADS + A_KV_HEADS
    return pl.pallas_call(
        functools.partial(_attn_a_kernel, kc=kc),
        grid=(nb, A_KV_HEADS, nq),
        in_specs=[pl.BlockSpec((tq, gw), lambda b, g, i: (b * nq + i, g)),
                  pl.BlockSpec((seq, HEAD_DIM), lambda b, g, i: (b, k_blk + g)),
                  pl.BlockSpec((seq, HEAD_DIM), lambda b, g, i: (b, v_blk + g)),
                  pl.BlockSpec((tq, HEAD_DIM), lambda b, g, i: (i, 0)),
                  pl.BlockSpec((tq, HEAD_DIM), lambda b, g, i: (i, 0)),
                  pl.BlockSpec((seq, HEAD_DIM), lambda b, g, i: (0, 0)),
                  pl.BlockSpec((seq, HEAD_DIM), lambda b, g, i: (0, 0)),
                  pl.BlockSpec((1, HEAD_DIM), lambda b, g, i: (0, 0)),
                  pl.BlockSpec((1, HEAD_DIM), lambda b, g, i: (0, 0))],
        out_specs=pl.BlockSpec((tq, gw), lambda b, g, i: (b * nq + i, g)),
        out_shape=jax.ShapeDtypeStruct((t, A_Q_HEADS * HEAD_DIM), BF16),
        scratch_shapes=[pltpu.VMEM((seq, HEAD_DIM), BF16)],
        compiler_params=_cparams("parallel", "parallel", "arbitrary"),
        name="mixer_a_axial_gqa",
    )(qkv, qkv, qkv, ctab, stab, ctab, stab, q_gain.reshape(1, HEAD_DIM), k_gain.reshape(1, HEAD_DIM))


def _window(q0, tq, half, seq):
    span = min(tq + 2 * half, seq)
    start = jnp.clip(q0 - half, 0, seq - span)
    return pl.multiple_of(start, 64), span


def _attn_b_kernel(q_ref, k_ref, v_ref, o_ref):
    tq, seq = q_ref.shape[0], k_ref.shape[0]
    q0 = pl.program_id(1) * tq
    scale = HEAD_DIM ** -0.5
    outs, lses = [], []
    for g, (win, r) in enumerate(B_GROUPS):
        half = win // 2
        start, span = _window(q0, tq, half, seq)
        qpos = q0 + lax.broadcasted_iota(jnp.int32, (tq, span), 0)
        kpos = start + lax.broadcasted_iota(jnp.int32, (tq, span), 1)
        dd = qpos - kpos
        ad = jnp.abs(dd)
        valid = (ad <= half) & ((dd & (r - 1)) == 0)
        adf = ad.astype(F32)
        for hh in range(B_HEADS_PER_GROUP):
            head = g * B_HEADS_PER_GROUP + hh
            cols = slice(head * HEAD_DIM, (head + 1) * HEAD_DIM)
            slope = float(2.0 ** (-8.0 * (head + 1) / B_HEADS))
            s = lax.dot_general(q_ref[:, cols], k_ref[pl.ds(start, span), cols],
                                (((1,), (1,)), ((), ())), preferred_element_type=F32) * scale
            s = jnp.where(valid, s - slope * adf, NEG_INF)
            m = jnp.max(s, axis=-1, keepdims=True)
            p = jnp.exp(s - m)
            l = jnp.sum(p, axis=-1, keepdims=True)
            o = jnp.dot(p.astype(BF16), v_ref[pl.ds(start, span), cols], preferred_element_type=F32)
            outs.append(o / l)
            lses.append(m + jnp.log(l))
    ng = len(B_GROUPS)
    for hh in range(B_HEADS_PER_GROUP):
        hl = [lses[g * B_HEADS_PER_GROUP + hh] for g in range(ng)]
        mx = functools.reduce(jnp.maximum, hl)
        ex = [jnp.exp(x - mx) for x in hl]
        tot = functools.reduce(lambda a, b: a + b, ex)
        for g in range(ng):
            head = g * B_HEADS_PER_GROUP + hh
            o_ref[:, head * HEAD_DIM:(head + 1) * HEAD_DIM] = (outs[head] * (ex[g] / tot)).astype(BF16)


def _mixer_b(qkv, nb, seq, tq=256):
    t = qkv.shape[0]
    nq = seq // tq
    bw = B_HEADS * HEAD_DIM
    return pl.pallas_call(
        _attn_b_kernel,
        grid=(nb, nq),
        in_specs=[pl.BlockSpec((tq, bw), lambda b, i: (b * nq + i, 0)),
                  pl.BlockSpec((seq, bw), lambda b, i: (b, 1)),
                  pl.BlockSpec((seq, bw), lambda b, i: (b, 2))],
        out_specs=pl.BlockSpec((tq, bw), lambda b, i: (b * nq + i, 0)),
        out_shape=jax.ShapeDtypeStruct((t, bw), BF16),
        compiler_params=_cparams("parallel", "arbitrary"),
        name="mixer_b_dilated",
    )(qkv, qkv, qkv)


def _attn_c_kernel(hp_ref, q_ref, k_ref, v_ref, o_ref):
    tq, seq = q_ref.shape[0], k_ref.shape[0]
    grp = q_ref.shape[1] // HEAD_DIM
    g = pl.program_id(1)
    q0 = pl.program_id(2) * tq
    start, span = _window(q0, tq, C_HALF_WINDOW, seq)
    qpos = q0 + lax.broadcasted_iota(jnp.int32, (tq, span), 0)
    kpos = start + lax.broadcasted_iota(jnp.int32, (tq, span), 1)
    ad = jnp.abs(qpos - kpos)
    valid = ad <= C_HALF_WINDOW
    adf = ad.astype(F32)
    kk = k_ref[pl.ds(start, span), :]
    vv = v_ref[pl.ds(start, span), :]
    scale = HEAD_DIM ** -0.5
    for hh in range(grp):
        cols = slice(hh * HEAD_DIM, (hh + 1) * HEAD_DIM)
        slope, sink = hp_ref[g, hh], hp_ref[g, grp + hh]
        s = lax.dot_general(q_ref[:, cols], kk, (((1,), (1,)), ((), ())),
                            preferred_element_type=F32) * scale
        s = jnp.where(valid, s - slope * adf, NEG_INF)
        m = jnp.maximum(jnp.max(s, axis=-1, keepdims=True), sink)
        p = jnp.exp(s - m)
        l = jnp.sum(p, axis=-1, keepdims=True) + jnp.exp(sink - m)
        o = jnp.dot(p.astype(BF16), vv, preferred_element_type=F32)
        o_ref[:, cols] = (o / l).astype(BF16)


def _mixer_c(qkv, sink, nb, seq, tq=256):
    t = qkv.shape[0]
    nq = seq // tq
    grp = C_Q_HEADS // C_KV_HEADS
    gw = grp * HEAD_DIM
    slopes = (2.0 ** (-8.0 * jnp.arange(1, C_Q_HEADS + 1, dtype=F32) / C_Q_HEADS)).reshape(C_KV_HEADS, grp)
    head_params = jnp.concatenate([slopes, sink.astype(F32)], axis=1)
    return pl.pallas_call(
        _attn_c_kernel,
        grid=(nb, C_KV_HEADS, nq),
        in_specs=[pl.BlockSpec(memory_space=pltpu.SMEM),
                  pl.BlockSpec((tq, gw), lambda b, g, i: (b * nq + i, g)),
                  pl.BlockSpec((seq, HEAD_DIM), lambda b, g, i: (b, C_Q_HEADS + g)),
                  pl.BlockSpec((seq, HEAD_DIM), lambda b, g, i: (b, C_Q_HEADS + C_KV_HEADS + g))],
        out_specs=pl.BlockSpec((tq, gw), lambda b, g, i: (b * nq + i, g)),
        out_shape=jax.ShapeDtypeStruct((t, C_Q_HEADS * HEAD_DIM), BF16),
        compiler_params=_cparams("parallel", "parallel", "arbitrary"),
        name="mixer_c_window_sink",
    )(head_params, qkv, qkv, qkv)


def _layer_norm(z, g, b):
    mu = jnp.mean(z, axis=-1, keepdims=True)
    zc = z - mu
    var = jnp.mean(zc * zc, axis=-1, keepdims=True)
    return zc * lax.rsqrt(var + LN_EPS) * g + b


def _merge_kernel(oa_ref, ob_ref, oc_ref, gl_ref, x_ref, ga_ref, scf_ref, shf_ref, lg_ref, lb_ref,
                  wa_ref, wb_ref, wc_ref, wo_ref, x1_ref, ht_ref, *, alpha):
    d = x_ref.shape[1]
    merged = None
    for i, (o_ref, w_ref) in enumerate(((oa_ref, wa_ref), (ob_ref, wb_ref), (oc_ref, wc_ref))):
        y = jnp.dot(o_ref[...], w_ref[...], preferred_element_type=F32)
        term = jax.nn.sigmoid(gl_ref[:, i * d:(i + 1) * d].astype(F32)) * y
        merged = term if merged is None else merged + term
    y = jnp.dot(merged.astype(BF16), wo_ref[...], preferred_element_type=F32)
    x1 = _layer_norm(alpha * x_ref[...] + (1.0 + ga_ref[0]) * y, lg_ref[...], lb_ref[...])
    x1_ref[...] = x1
    h2 = x1 * (1.0 + scf_ref[0]) + shf_ref[0]
    ht_ref[...] = h2.T.astype(BF16)


def _merge_project(oa, ob, oc, gl, x2d, mod3, ln_g, ln_b, wa, wb, wc, wo, seq, alpha, tm=256):
    t, d = x2d.shape
    per_b = seq // tm
    row = lambda w: pl.BlockSpec((tm, w), lambda i: (i, 0))
    whole = lambda a: pl.BlockSpec(a.shape, lambda i: (0, 0))
    modspec = lambda k: pl.BlockSpec((1, 1, d), lambda i: (i // per_b, 0, k))
    return pl.pallas_call(
        functools.partial(_merge_kernel, alpha=alpha),
        grid=(t // tm,),
        in_specs=[row(oa.shape[1]), row(ob.shape[1]), row(oc.shape[1]), row(gl.shape[1]), row(d),
                  modspec(2), modspec(4), modspec(3),
                  pl.BlockSpec((1, d), lambda i: (0, 0)), pl.BlockSpec((1, d), lambda i: (0, 0)),
                  whole(wa), whole(wb), whole(wc), whole(wo)],
        out_specs=[pl.BlockSpec((tm, d), lambda i: (i, 0)), pl.BlockSpec((d, tm), lambda i: (0, i))],
        out_shape=[jax.ShapeDtypeStruct((t, d), F32), jax.ShapeDtypeStruct((d, t), BF16)],
        compiler_params=_cparams("parallel"),
        name="merge_outproj_ln",
    )(oa, ob, oc, gl, x2d, mod3, mod3, mod3, ln_g.reshape(1, d), ln_b.reshape(1, d), wa, wb, wc, wo)


def _extract_top(s, n):
    rows = s.shape[0]
    ridx = lax.broadcasted_iota(jnp.int32, s.shape, 0)
    rank = jnp.full(s.shape, n, jnp.int32)
    vals = []
    for i in range(n):
        m = jnp.max(s, axis=0, keepdims=True)
        first = jnp.min(jnp.where(s == m, ridx, rows), axis=0, keepdims=True)
        hit = ridx == first
        rank = jnp.where(hit, i, rank)
        s = jnp.where(hit, -jnp.inf, s)
        vals.append(m)
    return vals, rank


def _route_kernel(ht_ref, wqt_ref, keys_ref, r2_ref, e2_ref, ja_ref, cf_ref):
    k = PEER_TOPK
    nk = PEER_N_KEYS
    qt = jnp.dot(wqt_ref[...], ht_ref[...], preferred_element_type=F32).astype(BF16)
    for h in range(PEER_HEADS):
        sc = [jnp.dot(keys_ref[p], qt[(2 * h + p) * nk:(2 * h + p + 1) * nk, :],
                      preferred_element_type=F32) for p in range(2)]
        v1, rank1 = _extract_top(sc[0], k)
        v2, rank2 = _extract_top(sc[1], k)
        v2s = jnp.concatenate(v2, axis=0)
        cand = jnp.concatenate([v1[i] + v2s for i in range(k)], axis=0)
        _, crank = _extract_top(cand, k)
        sel = crank < k
        e1s = jnp.exp(jnp.concatenate(v1, axis=0) - v1[0])
        e2s = jnp.exp(v2s - v2[0])
        prod = jnp.concatenate([e1s[i:i + 1] * e2s for i in range(k)], axis=0)
        z = jnp.sum(jnp.where(sel, prod, 0.0), axis=0, keepdims=True)
        self32 = sel.astype(F32)
        ja = jnp.zeros(rank1.shape, F32)
        for i in range(k):
            cnt = jnp.sum(self32[i * k:(i + 1) * k], axis=0, keepdims=True)
            ja = jnp.where(rank1 == i, cnt, ja)
        r2_ref[h] = rank2.astype(F32)
        e2_ref[h] = jnp.exp(sc[1] - v2[0])
        ja_ref[h] = ja
        cf_ref[h] = jnp.exp(sc[0] - v1[0]) / z


def _peer_route(ht, wqt, keys_bf, tm=256):
    d, t = ht.shape
    shp = jax.ShapeDtypeStruct((PEER_HEADS, PEER_N_KEYS, t), F32)
    spec = pl.BlockSpec((PEER_HEADS, PEER_N_KEYS, tm), lambda i: (0, 0, i))
    return pl.pallas_call(
        _route_kernel,
        grid=(t // tm,),
        in_specs=[pl.BlockSpec((d, tm), lambda i: (0, i)),
                  pl.BlockSpec(wqt.shape, lambda i: (0, 0)),
                  pl.BlockSpec(keys_bf.shape, lambda i: (0, 0, 0))],
        out_specs=[spec] * 4,
        out_shape=[shp] * 4,
        compiler_params=_cparams("parallel"),
        name="peer_route",
    )(ht, wqt, keys_bf)


def _experts_kernel(ht_ref, u_ref, vt_ref, r2_ref, e2_ref, ja_ref, cf_ref, y_ref, acc_ref):
    j = pl.program_id(1)
    nk = PEER_N_KEYS
    te = u_ref.shape[0]

    @pl.when(j == 0)
    def _():
        acc_ref[...] = jnp.zeros_like(acc_ref)

    act = jnp.dot(u_ref[...], ht_ref[...], preferred_element_type=F32)
    gel = 0.5 * act * (1.0 + lax.erf(act * (2.0 ** -0.5)))
    parts = []
    for aa in range(te // nk):
        a = j * (te // nk) + aa
        gate = None
        for h in range(PEER_HEADS):
            ja = ja_ref[h, pl.ds(a, 1), :]
            cf = cf_ref[h, pl.ds(a, 1), :]
            term = jnp.where(r2_ref[h] < ja, e2_ref[h] * cf, 0.0)
            gate = term if gate is None else gate + term
        parts.append((gate * gel[aa * nk:(aa + 1) * nk]).astype(BF16))
    wt = jnp.concatenate(parts, axis=0)
    acc_ref[...] += jnp.dot(vt_ref[...], wt, preferred_element_type=F32)

    @pl.when(j == pl.num_programs(1) - 1)
    def _():
        y_ref[...] = acc_ref[...].T


def _peer_experts(ht, u_bf, vt_bf, r2, e2, ja, cf, tm=512, te=512):
    d, t = ht.shape
    ne = u_bf.shape[0]
    side = pl.BlockSpec((PEER_HEADS, PEER_N_KEYS, tm), lambda i, j: (0, 0, i))
    return pl.pallas_call(
        _experts_kernel,
        grid=(t // tm, ne // te),
        in_specs=[pl.BlockSpec((d, tm), lambda i, j: (0, i)),
                  pl.BlockSpec((te, d), lambda i, j: (j, 0)),
                  pl.BlockSpec((d, te), lambda i, j: (0, j)),
                  side, side, side, side],
        out_specs=pl.BlockSpec((tm, d), lambda i, j: (i, 0)),
        out_shape=jax.ShapeDtypeStruct((t, d), F32),
        scratch_shapes=[pltpu.VMEM((d, tm), F32)],
        compiler_params=_cparams("parallel", "arbitrary"),
        name="peer_experts",
    )(ht, u_bf, vt_bf, r2, e2, ja, cf)


def _resid_ln_kernel(x_ref, y_ref, gf_ref, lg_ref, lb_ref, o_ref, *, alpha):
    o_ref[...] = _layer_norm(alpha * x_ref[...] + (1.0 + gf_ref[0]) * y_ref[...], lg_ref[...], lb_ref[...])


def _resid_ln(x2d, y, mod3, ln_g, ln_b, seq, alpha, tm=512):
    t, d = x2d.shape
    per_b = seq // tm
    row = pl.BlockSpec((tm, d), lambda i: (i, 0))
    vec = pl.BlockSpec((1, d), lambda i: (0, 0))
    return pl.pallas_call(
        functools.partial(_resid_ln_kernel, alpha=alpha),
        grid=(t // tm,),
        in_specs=[row, row, pl.BlockSpec((1, 1, d), lambda i: (i // per_b, 0, 5)), vec, vec],
        out_specs=row,
        out_shape=jax.ShapeDtypeStruct((t, d), F32),
        compiler_params=_cparams("parallel"),
        name="resid_ln",
    )(x2d, y, mod3, ln_g.reshape(1, d), ln_b.reshape(1, d))


def kernel(x, c, w_mod, b_mod, w_in, a_q_gain, a_k_gain, c_sink, w_pa, w_pb, w_pc, w_o,
           ln1_g, ln1_b, peer_wq, peer_keys, peer_u, peer_v, ln2_g, ln2_b):
    nb, seq, d = x.shape
    depth = w_mod.shape[0]
    alpha = float((2 * depth) ** 0.25)
    ctab, stab = _rope_tables(seq)
    mod = _modulation(c, w_mod, b_mod)
    x2d = x.reshape(nb * seq, d)
    for l in range(depth):
        mod3 = mod[l].reshape(nb, 1, 6 * d)
        qkv_a, qkv_b, qkv_c, gl = _in_projection(x2d, mod3, w_in[l].astype(BF16), seq)
        oa = _mixer_a(qkv_a, ctab, stab, a_q_gain[l], a_k_gain[l], nb, seq)
        ob = _mixer_b(qkv_b, nb, seq)
        oc = _mixer_c(qkv_c, c_sink[l], nb, seq)
        x1, ht = _merge_project(oa, ob, oc, gl, x2d, mod3, ln1_g[l], ln1_b[l],
                                w_pa[l].astype(BF16), w_pb[l].astype(BF16), w_pc[l].astype(BF16),
                                w_o[l].astype(BF16), seq, alpha)
        r2, e2, ja, cf = _peer_route(ht, peer_wq[l].T.astype(BF16), peer_keys[l].astype(BF16))
        y = _peer_experts(ht, peer_u[l].astype(BF16), peer_v[l].T.astype(BF16), r2, e2, ja, cf)
        x2d = _resid_ln(x1, y, mod3, ln2_g[l], ln2_b[l], seq, alpha)
    return x2d.reshape(nb, seq, d)
```

```python
import functools

import numpy as np
import jax
import jax.numpy as jnp
from jax import lax
from jax.experimental import pallas as pl
from jax.experimental.pallas import tpu as pltpu

F32 = jnp.float32
BF16 = jnp.bfloat16

HEAD_DIM = 128
A_Q_HEADS, A_KV_HEADS = 8, 2
B_GROUPS = ((128, 1), (512, 4), (2048, 16))
B_HEADS_PER_GROUP = 2
B_HEADS = B_HEADS_PER_GROUP * len(B_GROUPS)
C_Q_HEADS, C_KV_HEADS = 8, 2
C_HALF_WINDOW = 128
N_BRANCHES = 3
GRID_W = 64
ROPE_THETA = 10000.0
PEER_HEADS = 8
PEER_N_KEYS = 128
PEER_TOPK = 16
LN_EPS = 1e-5
QK_EPS = 1e-6
NEG_INF = -1e30

LANES = 128
VMEM_LIMIT = 56 * 1024 * 1024


def _cparams(*sem):
    return pltpu.CompilerParams(dimension_semantics=sem, vmem_limit_bytes=VMEM_LIMIT)


def _mod_kernel(cb_ref, w_ref, b_ref, o_ref):
    nb, tn = cb_ref.shape[0], w_ref.shape[2]
    rows = []
    for b in range(nb):
        cb = cb_ref[b]
        pieces = [jnp.sum(w_ref[0, :, n * LANES:(n + 1) * LANES] * cb, axis=0, keepdims=True)
                  for n in range(tn // LANES)]
        rows.append(jnp.concatenate(pieces, axis=1))
    o_ref[0] = jnp.concatenate(rows, axis=0) + b_ref[0]


def _modulation(c, w_mod, b_mod, tn=1024):
    nl, d, n6 = w_mod.shape
    nb = c.shape[0]
    cb = jnp.broadcast_to(c[:, :, None], (nb, d, LANES))
    return pl.pallas_call(
        _mod_kernel,
        grid=(nl, n6 // tn),
        in_specs=[pl.BlockSpec((nb, d, LANES), lambda l, j: (0, 0, 0)),
                  pl.BlockSpec((1, d, tn), lambda l, j: (l, 0, j)),
                  pl.BlockSpec((1, 1, tn), lambda l, j: (l, 0, j))],
        out_specs=pl.BlockSpec((1, nb, tn), lambda l, j: (l, 0, j)),
        out_shape=jax.ShapeDtypeStruct((nl, nb, n6), F32),
        compiler_params=_cparams("parallel", "parallel"),
        name="adaln_modulation",
    )(cb, w_mod, b_mod.reshape(nl, 1, n6))


IN_TN = 768
IN_TILES = (2, 3, 2, 8)


def _inproj_kernel(x_ref, sc_ref, sh_ref, w_ref, oa_ref, ob_ref, oc_ref, og_ref, h_scr):
    j = pl.program_id(1)

    @pl.when(j == 0)
    def _():
        h_scr[...] = (x_ref[...] * (1.0 + sc_ref[0]) + sh_ref[0]).astype(BF16)

    res = jnp.dot(h_scr[...], w_ref[...], preferred_element_type=F32).astype(BF16)
    lo = 0
    for ref, n in zip((oa_ref, ob_ref, oc_ref, og_ref), IN_TILES):
        @pl.when((j >= lo) & (j < lo + n))
        def _(ref=ref):
            ref[...] = res
        lo += n


def _in_projection(x2d, mod3, w_bf, seq, tm=1024):
    t, d = x2d.shape
    tm = min(tm, seq)
    per_b = seq // tm
    starts = np.cumsum((0,) + IN_TILES[:-1]).tolist()

    def out_spec(lo, n):
        return pl.BlockSpec((tm, IN_TN), lambda i, j: (i, jnp.clip(j - lo, 0, n - 1)))

    return pl.pallas_call(
        _inproj_kernel,
        grid=(t // tm, sum(IN_TILES)),
        in_specs=[pl.BlockSpec((tm, d), lambda i, j: (i, 0)),
                  pl.BlockSpec((1, 1, d), lambda i, j: (i // per_b, 0, 1)),
                  pl.BlockSpec((1, 1, d), lambda i, j: (i // per_b, 0, 0)),
                  pl.BlockSpec((d, IN_TN), lambda i, j: (0, j))],
        out_specs=[out_spec(lo, n) for lo, n in zip(starts, IN_TILES)],
        out_shape=[jax.ShapeDtypeStruct((t, n * IN_TN), BF16) for n in IN_TILES],
        scratch_shapes=[pltpu.VMEM((tm, d), BF16)],
        compiler_params=_cparams("parallel", "arbitrary"),
        name="in_projection",
    )(x2d, mod3, mod3, w_bf)


def _rope_tables(seq):
    rows = seq // GRID_W
    row = jnp.repeat(jnp.arange(rows, dtype=F32), GRID_W)
    col = jnp.tile(jnp.arange(GRID_W, dtype=F32), rows)
    quarter = HEAD_DIM // 4
    inv = ROPE_THETA ** (-jnp.arange(quarter, dtype=F32) / quarter)
    ar, ac = row[:, None] * inv, col[:, None] * inv
    ctab = jnp.concatenate([jnp.cos(ar), jnp.cos(ar), jnp.cos(ac), jnp.cos(ac)], axis=1)
    stab = jnp.concatenate([-jnp.sin(ar), jnp.sin(ar), -jnp.sin(ac), jnp.sin(ac)], axis=1)
    return ctab, stab


def _norm_rope(xf, gain, ctab, stab):
    ms = jnp.mean(xf * xf, axis=-1, keepdims=True)
    xn = xf * lax.rsqrt(ms + QK_EPS) * gain
    lane = lax.broadcasted_iota(jnp.int32, xn.shape, 1)
    partner = jnp.where((lane & 63) < 32, pltpu.roll(xn, 96, axis=1), pltpu.roll(xn, 32, axis=1))
    return xn * ctab + partner * stab


def _attn_a_kernel(q_ref, k_ref, v_ref, cq_ref, sq_ref, ck_ref, sk_ref, qg_ref, kg_ref,
                   o_ref, kp_scr, *, kc):
    tq, seq = q_ref.shape[0], k_ref.shape[0]
    grp = q_ref.shape[1] // HEAD_DIM

    @pl.when(pl.program_id(2) == 0)
    def _():
        kp_scr[...] = _norm_rope(k_ref[...].astype(F32), kg_ref[...], ck_ref[...], sk_ref[...]).astype(BF16)

    cq, sq, qg = cq_ref[...], sq_ref[...], qg_ref[...]
    scale = HEAD_DIM ** -0.5
    q4 = jnp.concatenate(
        [(_norm_rope(q_ref[:, h * HEAD_DIM:(h + 1) * HEAD_DIM].astype(F32), qg, cq, sq) * scale).astype(BF16)
         for h in range(grp)], axis=0)

    def body(c, carry):
        m, l, acc = carry
        k0 = pl.multiple_of(c * kc, kc)
        s = lax.dot_general(q4, kp_scr[pl.ds(k0, kc), :], (((1,), (1,)), ((), ())),
                            preferred_element_type=F32)
        mn = jnp.maximum(m, jnp.max(s, axis=-1, keepdims=True))
        a = jnp.exp(m - mn)
        p = jnp.exp(s - mn)
        l = a * l + jnp.sum(p, axis=-1, keepdims=True)
        acc = a * acc + jnp.dot(p.astype(BF16), v_ref[pl.ds(k0, kc), :], preferred_element_type=F32)
        return mn, l, acc

    rows = grp * tq
    m, l, acc = lax.fori_loop(0, seq // kc, body,
                              (jnp.full((rows, 1), -jnp.inf, F32), jnp.zeros((rows, 1), F32),
                               jnp.zeros((rows, HEAD_DIM), F32)))
    o = acc / l
    for h in range(grp):
        o_ref[:, h * HEAD_DIM:(h + 1) * HEAD_DIM] = o[h * tq:(h + 1) * tq].astype(BF16)


def _mixer_a(qkv, ctab, stab, q_gain, k_gain, nb, seq, tq=256, kc=512):
    t = qkv.shape[0]
    nq = seq // tq
    gw = (A_Q_HEADS // A_KV_HEADS) * HEAD_DIM
    k_blk = A_Q_HEADS
    v_blk = A_Q_HEADS + A_KV_HEADS
    return pl.pallas_call(
        functools.partial(_attn_a_kernel, kc=kc),
        grid=(nb, A_KV_HEADS, nq),
        in_specs=[pl.BlockSpec((tq, gw), lambda b, g, i: (b * nq + i, g)),
                  pl.BlockSpec((seq, HEAD_DIM), lambda b, g, i: (b, k_blk + g)),
                  pl.BlockSpec((seq, HEAD_DIM), lambda b, g, i: (b, v_blk + g)),
                  pl.BlockSpec((tq, HEAD_DIM), lambda b, g, i: (i, 0)),
                  pl.BlockSpec((tq, HEAD_DIM), lambda b, g, i: (i, 0)),
                  pl.BlockSpec((seq, HEAD_DIM), lambda b, g, i: (0, 0)),
                  pl.BlockSpec((seq, HEAD_DIM), lambda b, g, i: (0, 0)),
                  pl.BlockSpec((1, HEAD_DIM), lambda b, g, i: (0, 0)),
                  pl.BlockSpec((1, HEAD_DIM), lambda b, g, i: (0, 0))],
        out_specs=pl.BlockSpec((tq, gw), lambda b, g, i: (b * nq + i, g)),
        out_shape=jax.ShapeDtypeStruct((t, A_Q_HEADS * HEAD_DIM), BF16),
        scratch_shapes=[pltpu.VMEM((seq, HEAD_DIM), BF16)],
        compiler_params=_cparams("parallel", "parallel", "arbitrary"),
        name="mixer_a_axial_gqa",
    )(qkv, qkv, qkv, ctab, stab, ctab, stab, q_gain.reshape(1, HEAD_DIM), k_gain.reshape(1, HEAD_DIM))


def _window(q0, tq, half, seq):
    span = min(tq + 2 * half, seq)
    start = jnp.clip(q0 - half, 0, seq - span)
    return pl.multiple_of(start, 64), span


def _attn_b_kernel(q_ref, k_ref, v_ref, o_ref):
    tq, seq = q_ref.shape[0], k_ref.shape[0]
    q0 = pl.program_id(1) * tq
    scale = HEAD_DIM ** -0.5
    outs, lses = [], []
    for g, (win, r) in enumerate(B_GROUPS):
        half = win // 2
        start, span = _window(q0, tq, half, seq)
        qpos = q0 + lax.broadcasted_iota(jnp.int32, (tq, span), 0)
        kpos = start + lax.broadcasted_iota(jnp.int32, (tq, span), 1)
        dd = qpos - kpos
        ad = jnp.abs(dd)
        valid = (ad <= half) & ((dd & (r - 1)) == 0)
        adf = ad.astype(F32)
        for hh in range(B_HEADS_PER_GROUP):
            head = g * B_HEADS_PER_GROUP + hh
            cols = slice(head * HEAD_DIM, (head + 1) * HEAD_DIM)
            slope = float(2.0 ** (-8.0 * (head + 1) / B_HEADS))
            s = lax.dot_general(q_ref[:, cols], k_ref[pl.ds(start, span), cols],
                                (((1,), (1,)), ((), ())), preferred_element_type=F32) * scale
            s = jnp.where(valid, s - slope * adf, NEG_INF)
            m = jnp.max(s, axis=-1, keepdims=True)
            p = jnp.exp(s - m)
            l = jnp.sum(p, axis=-1, keepdims=True)
            o = jnp.dot(p.astype(BF16), v_ref[pl.ds(start, span), cols], preferred_element_type=F32)
            outs.append(o / l)
            lses.append(m + jnp.log(l))
    ng = len(B_GROUPS)
    for hh in range(B_HEADS_PER_GROUP):
        hl = [lses[g * B_HEADS_PER_GROUP + hh] for g in range(ng)]
        mx = functools.reduce(jnp.maximum, hl)
        ex = [jnp.exp(x - mx) for x in hl]
        tot = functools.reduce(lambda a, b: a + b, ex)
        for g in range(ng):
            head = g * B_HEADS_PER_GROUP + hh
            o_ref[:, head * HEAD_DIM:(head + 1) * HEAD_DIM] = (outs[head] * (ex[g] / tot)).astype(BF16)


def _mixer_b(qkv, nb, seq, tq=256):
    t = qkv.shape[0]
    nq = seq // tq
    bw = B_HEADS * HEAD_DIM
    return pl.pallas_call(
        _attn_b_kernel,
        grid=(nb, nq),
        in_specs=[pl.BlockSpec((tq, bw), lambda b, i: (b * nq + i, 0)),
                  pl.BlockSpec((seq, bw), lambda b, i: (b, 1)),
                  pl.BlockSpec((seq, bw), lambda b, i: (b, 2))],
        out_specs=pl.BlockSpec((tq, bw), lambda b, i: (b * nq + i, 0)),
        out_shape=jax.ShapeDtypeStruct((t, bw), BF16),
        compiler_params=_cparams("parallel", "arbitrary"),
        name="mixer_b_dilated",
    )(qkv, qkv, qkv)


def _attn_c_kernel(hp_ref, q_ref, k_ref, v_ref, o_ref):
    tq, seq = q_ref.shape[0], k_ref.shape[0]
    grp = q_ref.shape[1] // HEAD_DIM
    g = pl.program_id(1)
    q0 = pl.program_id(2) * tq
    start, span = _window(q0, tq, C_HALF_WINDOW, seq)
    qpos = q0 + lax.broadcasted_iota(jnp.int32, (tq, span), 0)
    kpos = start + lax.broadcasted_iota(jnp.int32, (tq, span), 1)
    ad = jnp.abs(qpos - kpos)
    valid = ad <= C_HALF_WINDOW
    adf = ad.astype(F32)
    kk = k_ref[pl.ds(start, span), :]
    vv = v_ref[pl.ds(start, span), :]
    scale = HEAD_DIM ** -0.5
    for hh in range(grp):
        cols = slice(hh * HEAD_DIM, (hh + 1) * HEAD_DIM)
        slope, sink = hp_ref[g, hh], hp_ref[g, grp + hh]
        s = lax.dot_general(q_ref[:, cols], kk, (((1,), (1,)), ((), ())),
                            preferred_element_type=F32) * scale
        s = jnp.where(valid, s - slope * adf, NEG_INF)
        m = jnp.maximum(jnp.max(s, axis=-1, keepdims=True), sink)
        p = jnp.exp(s - m)
        l = jnp.sum(p, axis=-1, keepdims=True) + jnp.exp(sink - m)
        o = jnp.dot(p.astype(BF16), vv, preferred_element_type=F32)
        o_ref[:, cols] = (o / l).astype(BF16)


def _mixer_c(qkv, sink, nb, seq, tq=256):
    t = qkv.shape[0]
    nq = seq // tq
    grp = C_Q_HEADS // C_KV_HEADS
    gw = grp * HEAD_DIM
    slopes = (2.0 ** (-8.0 * jnp.arange(1, C_Q_HEADS + 1, dtype=F32) / C_Q_HEADS)).reshape(C_KV_HEADS, grp)
    head_params = jnp.concatenate([slopes, sink.astype(F32)], axis=1)
    return pl.pallas_call(
        _attn_c_kernel,
        grid=(nb, C_KV_HEADS, nq),
        in_specs=[pl.BlockSpec(memory_space=pltpu.SMEM),
                  pl.BlockSpec((tq, gw), lambda b, g, i: (b * nq + i, g)),
                  pl.BlockSpec((seq, HEAD_DIM), lambda b, g, i: (b, C_Q_HEADS + g)),
                  pl.BlockSpec((seq, HEAD_DIM), lambda b, g, i: (b, C_Q_HEADS + C_KV_HEADS + g))],
        out_specs=pl.BlockSpec((tq, gw), lambda b, g, i: (b * nq + i, g)),
        out_shape=jax.ShapeDtypeStruct((t, C_Q_HEADS * HEAD_DIM), BF16),
        compiler_params=_cparams("parallel", "parallel", "arbitrary"),
        name="mixer_c_window_sink",
    )(head_params, qkv, qkv, qkv)


def _layer_norm(z, g, b):
    mu = jnp.mean(z, axis=-1, keepdims=True)
    zc = z - mu
    var = jnp.mean(zc * zc, axis=-1, keepdims=True)
    return zc * lax.rsqrt(var + LN_EPS) * g + b


def _merge_kernel(oa_ref, ob_ref, oc_ref, gl_ref, x_ref, ga_ref, scf_ref, shf_ref, lg_ref, lb_ref,
                  wa_ref, wb_ref, wc_ref, wo_ref, x1_ref, ht_ref, *, alpha):
    d = x_ref.shape[1]
    merged = None
    for i, (o_ref, w_ref) in enumerate(((oa_ref, wa_ref), (ob_ref, wb_ref), (oc_ref, wc_ref))):
        y = jnp.dot(o_ref[...], w_ref[...], preferred_element_type=F32)
        term = jax.nn.sigmoid(gl_ref[:, i * d:(i + 1) * d].astype(F32)) * y
        merged = term if merged is None else merged + term
    y = jnp.dot(merged.astype(BF16), wo_ref[...], preferred_element_type=F32)
    x1 = _layer_norm(alpha * x_ref[...] + (1.0 + ga_ref[0]) * y, lg_ref[...], lb_ref[...])
    x1_ref[...] = x1
    h2 = x1 * (1.0 + scf_ref[0]) + shf_ref[0]
    ht_ref[...] = h2.T.astype(BF16)


def _merge_project(oa, ob, oc, gl, x2d, mod3, ln_g, ln_b, wa, wb, wc, wo, seq, alpha, tm=256):
    t, d = x2d.shape
    per_b = seq // tm
    row = lambda w: pl.BlockSpec((tm, w), lambda i: (i, 0))
    whole = lambda a: pl.BlockSpec(a.shape, lambda i: (0, 0))
    modspec = lambda k: pl.BlockSpec((1, 1, d), lambda i: (i // per_b, 0, k))
    return pl.pallas_call(
        functools.partial(_merge_kernel, alpha=alpha),
        grid=(t // tm,),
        in_specs=[row(oa.shape[1]), row(ob.shape[1]), row(oc.shape[1]), row(gl.shape[1]), row(d),
                  modspec(2), modspec(4), modspec(3),
                  pl.BlockSpec((1, d), lambda i: (0, 0)), pl.BlockSpec((1, d), lambda i: (0, 0)),
                  whole(wa), whole(wb), whole(wc), whole(wo)],
        out_specs=[pl.BlockSpec((tm, d), lambda i: (i, 0)), pl.BlockSpec((d, tm), lambda i: (0, i))],
        out_shape=[jax.ShapeDtypeStruct((t, d), F32), jax.ShapeDtypeStruct((d, t), BF16)],
        compiler_params=_cparams("parallel"),
        name="merge_outproj_ln",
    )(oa, ob, oc, gl, x2d, mod3, mod3, mod3, ln_g.reshape(1, d), ln_b.reshape(1, d), wa, wb, wc, wo)


CAND_ROWS = 4
ROUTE_GROUPS = 1


def _top_exact(s, n, order):
    nrow = lax.broadcasted_iota(jnp.int32, (n, s.shape[1]), 0)

    def step(i, carry):
        s, rank, vals = carry
        m = jnp.max(s, axis=0, keepdims=True)
        first = jnp.min(jnp.where(s == m, order, jnp.iinfo(jnp.int32).max), axis=0, keepdims=True)
        hit = order == first
        return jnp.where(hit, -jnp.inf, s), jnp.where(hit, i, rank), jnp.where(nrow == i, m, vals)

    _, rank, vals = lax.fori_loop(0, n, step, (s, jnp.full(s.shape, n, jnp.int32), jnp.zeros(nrow.shape, F32)))
    return vals, rank


def _top_fast(ss, n, want_rank):
    nrow = lax.broadcasted_iota(jnp.int32, (n, ss[0].shape[1]), 0)
    ss = list(ss)
    vals = [jnp.zeros(nrow.shape, F32) for _ in ss]
    ranks = [jnp.full(s.shape, n, jnp.int32) if w else None for s, w in zip(ss, want_rank)]
    for i in range(n):
        for a in range(len(ss)):
            m = jnp.max(ss[a], axis=0, keepdims=True)
            hit = ss[a] == m
            vals[a] = jnp.where(nrow == i, m, vals[a])
            if ranks[a] is not None:
                ranks[a] = jnp.where(hit, i, ranks[a])
            ss[a] = jnp.where(hit, -jnp.inf, ss[a])
    return vals, ranks


def _candidates(v1, v2):
    k, c = PEER_TOPK, CAND_ROWS
    ki = lax.broadcasted_iota(jnp.int32, v1.shape, 0)
    v1_tail = jnp.where(ki >= c, v1, -jnp.inf)
    cand = jnp.concatenate([v1[i:i + 1] + v2 for i in range(c)] + [v2[j:j + 1] + v1_tail for j in range(c)], axis=0)
    flat = jnp.concatenate([ki + i * k for i in range(c)] +
                           [jnp.where(ki >= c, ki * k + j, -1) for j in range(c)], axis=0)
    return cand, flat


def _route_outputs(s1, s2, v1, v2, rank2, sel, is_ith):
    k, c = PEER_TOPK, CAND_ROWS
    e1 = jnp.exp(v1 - v1[0:1])
    e2 = jnp.exp(v2 - v2[0:1])
    prod = jnp.concatenate([e1[i:i + 1] * e2 for i in range(c)] + [e2[j:j + 1] * e1 for j in range(c)], axis=0)
    z = jnp.sum(jnp.where(sel, prod, 0.0), axis=0, keepdims=True)
    self32 = sel.astype(F32)
    tail = functools.reduce(lambda a, b: a + b, [self32[(c + j) * k:(c + j + 1) * k] for j in range(c)])
    ja = jnp.zeros(s1.shape, F32)
    for i in range(k):
        cnt = jnp.sum(self32[i * k:(i + 1) * k], axis=0, keepdims=True) if i < c else tail[i:i + 1]
        ja = jnp.where(is_ith(i), cnt, ja)
    return (rank2.astype(F32).astype(BF16), jnp.exp(s2 - v2[0:1]).astype(BF16), ja,
            jnp.exp(s1 - v1[0:1]) / z)


def _route_exact(s1, s2):
    k = PEER_TOPK
    rows = lax.broadcasted_iota(jnp.int32, s1.shape, 0)
    v1, rank1 = _top_exact(s1, k, rows)
    v2, rank2 = _top_exact(s2, k, rows)
    cand, flat = _candidates(v1, v2)
    _, crank = _top_exact(cand, k, flat)
    return _route_outputs(s1, s2, v1, v2, rank2, crank < k, lambda i: rank1 == i)


def _route_fast(s1s, s2s):
    k, n = PEER_TOPK, len(s1s)
    vals, ranks = _top_fast(list(s1s) + list(s2s), k, [False] * n + [True] * n)
    cands = [_candidates(vals[g], vals[n + g])[0] for g in range(n)]
    cvals, _ = _top_fast(cands, k, [False] * n)
    outs = []
    for g in range(n):
        s1, s2, v1, v2 = s1s[g], s2s[g], vals[g], vals[n + g]
        sel = cands[g] >= cvals[g][k - 1:k]

        def count(mask):
            return jnp.sum(mask.astype(jnp.int32), axis=0, keepdims=True)

        tie = (count(s1 >= v1[k - 1:k]) != k) | (count(ranks[n + g] < k) != k) | (count(sel) != k)
        res = _route_outputs(s1, s2, v1, v2, ranks[n + g], sel, lambda i, s1=s1, v1=v1: s1 == v1[i:i + 1])
        outs.append((res, tie.astype(jnp.int32)))
    return outs


def _route_kernel(ht_ref, wqt_ref, keys_ref, r2_ref, e2_ref, ja_ref, cf_ref, qt_scr):
    nk = PEER_N_KEYS
    tm = ht_ref.shape[1]
    qt_scr[...] = jnp.dot(wqt_ref[...], ht_ref[...], preferred_element_type=F32).astype(BF16)

    def head(h, carry):
        q0 = pl.multiple_of(h * 2 * nk, 2 * nk)
        s1 = jnp.dot(keys_ref[0], qt_scr[pl.ds(q0, nk), :], preferred_element_type=F32)
        s2 = jnp.dot(keys_ref[1], qt_scr[pl.ds(q0 + nk, nk), :], preferred_element_type=F32)
        def store(cols, res):
            r2_ref[h, :, cols], e2_ref[h, :, cols], ja_ref[h, :, cols], cf_ref[h, :, cols] = res

        for g0 in range(0, tm // LANES, ROUTE_GROUPS):
            colss = [slice(g * LANES, (g + 1) * LANES) for g in range(g0, g0 + ROUTE_GROUPS)]
            outs = _route_fast([s1[:, c] for c in colss], [s2[:, c] for c in colss])
            for cols, (res, tie) in zip(colss, outs):
                store(cols, res)

                @pl.when(jnp.max(tie) > 0)
                def _(cols=cols):
                    store(cols, _route_exact(s1[:, cols], s2[:, cols]))
        return carry

    lax.fori_loop(0, PEER_HEADS, head, 0)


def _peer_route(ht, wqt, keys_bf, tm=512):
    d, t = ht.shape
    shp = lambda dt: jax.ShapeDtypeStruct((PEER_HEADS, PEER_N_KEYS, t), dt)
    spec = pl.BlockSpec((PEER_HEADS, PEER_N_KEYS, tm), lambda i: (0, 0, i))
    return pl.pallas_call(
        _route_kernel,
        grid=(t // tm,),
        in_specs=[pl.BlockSpec((d, tm), lambda i: (0, i)),
                  pl.BlockSpec(wqt.shape, lambda i: (0, 0)),
                  pl.BlockSpec(keys_bf.shape, lambda i: (0, 0, 0))],
        out_specs=[spec] * 4,
        out_shape=[shp(BF16), shp(BF16), shp(F32), shp(F32)],
        scratch_shapes=[pltpu.VMEM((wqt.shape[0], tm), BF16)],
        compiler_params=_cparams("parallel"),
        name="peer_route",
    )(ht, wqt, keys_bf)


EXP_CHUNK = 512

def _experts_kernel(ht_ref, u_ref, vt_ref, r2_ref, e2_ref, ja_ref, cf_ref, y_ref, acc_ref):
    j = pl.program_id(1)
    nk = PEER_N_KEYS
    te = u_ref.shape[0]

    @pl.when(j == 0)
    def _():
        acc_ref[...] = jnp.zeros_like(acc_ref)

    zero = jnp.zeros((), BF16)
    ht = ht_ref[...]
    nchunk = te // EXP_CHUNK

    def hidden(c):
        return jnp.dot(u_ref[c * EXP_CHUNK:(c + 1) * EXP_CHUNK, :], ht, preferred_element_type=F32)

    act = hidden(0)
    for c in range(nchunk):
        nxt = hidden(c + 1) if c + 1 < nchunk else None
        gel = (0.5 * act * (1.0 + lax.erf(act * (2.0 ** -0.5)))).astype(BF16)
        parts = []
        for aa in range(EXP_CHUNK // nk):
            a = j * (te // nk) + c * (EXP_CHUNK // nk) + aa
            gate = None
            for h in range(PEER_HEADS):
                ja = ja_ref[h, pl.ds(a, 1), :].astype(BF16)
                cf = cf_ref[h, pl.ds(a, 1), :].astype(BF16)
                term = jnp.where(r2_ref[h] < ja, e2_ref[h] * cf, zero)
                gate = term if gate is None else gate + term
            parts.append(gate * gel[aa * nk:(aa + 1) * nk])
        wt = jnp.concatenate(parts, axis=0)
        acc_ref[...] += jnp.dot(vt_ref[:, c * EXP_CHUNK:(c + 1) * EXP_CHUNK], wt, preferred_element_type=F32)
        act = nxt

    @pl.when(j == pl.num_programs(1) - 1)
    def _():
        y_ref[...] = acc_ref[...].T


def _peer_experts(ht, u_bf, vt_bf, r2, e2, ja, cf, tm=512, te=1024):
    d, t = ht.shape
    ne = u_bf.shape[0]
    side = pl.BlockSpec((PEER_HEADS, PEER_N_KEYS, tm), lambda i, j: (0, 0, i))
    return pl.pallas_call(
        _experts_kernel,
        grid=(t // tm, ne // te),
        in_specs=[pl.BlockSpec((d, tm), lambda i, j: (0, i)),
                  pl.BlockSpec((te, d), lambda i, j: (j, 0)),
                  pl.BlockSpec((d, te), lambda i, j: (0, j)),
                  side, side, side, side],
        out_specs=pl.BlockSpec((tm, d), lambda i, j: (i, 0)),
        out_shape=jax.ShapeDtypeStruct((t, d), F32),
        scratch_shapes=[pltpu.VMEM((d, tm), F32)],
        compiler_params=_cparams("parallel", "arbitrary"),
        name="peer_experts",
    )(ht, u_bf, vt_bf, r2, e2, ja, cf)


def _resid_ln_kernel(x_ref, y_ref, gf_ref, lg_ref, lb_ref, o_ref, *, alpha):
    o_ref[...] = _layer_norm(alpha * x_ref[...] + (1.0 + gf_ref[0]) * y_ref[...], lg_ref[...], lb_ref[...])


def _resid_ln(x2d, y, mod3, ln_g, ln_b, seq, alpha, tm=512):
    t, d = x2d.shape
    per_b = seq // tm
    row = pl.BlockSpec((tm, d), lambda i: (i, 0))
    vec = pl.BlockSpec((1, d), lambda i: (0, 0))
    return pl.pallas_call(
        functools.partial(_resid_ln_kernel, alpha=alpha),
        grid=(t // tm,),
        in_specs=[row, row, pl.BlockSpec((1, 1, d), lambda i: (i // per_b, 0, 5)), vec, vec],
        out_specs=row,
        out_shape=jax.ShapeDtypeStruct((t, d), F32),
        compiler_params=_cparams("parallel"),
        name="resid_ln",
    )(x2d, y, mod3, ln_g.reshape(1, d), ln_b.reshape(1, d))


def kernel(x, c, w_mod, b_mod, w_in, a_q_gain, a_k_gain, c_sink, w_pa, w_pb, w_pc, w_o,
           ln1_g, ln1_b, peer_wq, peer_keys, peer_u, peer_v, ln2_g, ln2_b):
    nb, seq, d = x.shape
    depth = w_mod.shape[0]
    alpha = float((2 * depth) ** 0.25)
    ctab, stab = _rope_tables(seq)
    mod = _modulation(c, w_mod, b_mod)
    x2d = x.reshape(nb * seq, d)
    for l in range(depth):
        mod3 = mod[l].reshape(nb, 1, 6 * d)
        qkv_a, qkv_b, qkv_c, gl = _in_projection(x2d, mod3, w_in[l].astype(BF16), seq)
        oa = _mixer_a(qkv_a, ctab, stab, a_q_gain[l], a_k_gain[l], nb, seq)
        ob = _mixer_b(qkv_b, nb, seq)
        oc = _mixer_c(qkv_c, c_sink[l], nb, seq)
        x1, ht = _merge_project(oa, ob, oc, gl, x2d, mod3, ln1_g[l], ln1_b[l],
                                w_pa[l].astype(BF16), w_pb[l].astype(BF16), w_pc[l].astype(BF16),
                                w_o[l].astype(BF16), seq, alpha)
        r2, e2, ja, cf = _peer_route(ht, peer_wq[l].T.astype(BF16), peer_keys[l].astype(BF16))
        y = _peer_experts(ht, peer_u[l].astype(BF16), peer_v[l].T.astype(BF16), r2, e2, ja, cf)
        x2d = _resid_ln(x1, y, mod3, ln2_g[l], ln2_b[l], seq, alpha)
    return x2d.reshape(nb, seq, d)
```

```python
import functools

import numpy as np
import jax
import jax.numpy as jnp
from jax import lax
from jax.experimental import pallas as pl
from jax.experimental.pallas import tpu as pltpu

F32 = jnp.float32
BF16 = jnp.bfloat16

HEAD_DIM = 128
A_Q_HEADS, A_KV_HEADS = 8, 2
B_GROUPS = ((128, 1), (512, 4), (2048, 16))
B_HEADS_PER_GROUP = 2
B_HEADS = B_HEADS_PER_GROUP * len(B_GROUPS)
C_Q_HEADS, C_KV_HEADS = 8, 2
C_HALF_WINDOW = 128
N_BRANCHES = 3
GRID_W = 64
ROPE_THETA = 10000.0
PEER_HEADS = 8
PEER_N_KEYS = 128
PEER_TOPK = 16
LN_EPS = 1e-5
QK_EPS = 1e-6
NEG_INF = -1e30

LANES = 128
VMEM_LIMIT = 56 * 1024 * 1024
MERGE_VMEM_LIMIT = 58 * 1024 * 1024


def _cparams(*sem, vmem=VMEM_LIMIT):
    return pltpu.CompilerParams(dimension_semantics=sem, vmem_limit_bytes=vmem)


def _mod_kernel(cb_ref, w_ref, b_ref, o_ref):
    nb, tn = cb_ref.shape[0], w_ref.shape[2]
    rows = []
    for b in range(nb):
        cb = cb_ref[b]
        pieces = [jnp.sum(w_ref[0, :, n * LANES:(n + 1) * LANES] * cb, axis=0, keepdims=True)
                  for n in range(tn // LANES)]
        rows.append(jnp.concatenate(pieces, axis=1))
    o_ref[0] = jnp.concatenate(rows, axis=0) + b_ref[0]


def _modulation(c, w_mod, b_mod, tn=1024):
    nl, d, n6 = w_mod.shape
    nb = c.shape[0]
    cb = jnp.broadcast_to(c[:, :, None], (nb, d, LANES))
    return pl.pallas_call(
        _mod_kernel,
        grid=(nl, n6 // tn),
        in_specs=[pl.BlockSpec((nb, d, LANES), lambda l, j: (0, 0, 0)),
                  pl.BlockSpec((1, d, tn), lambda l, j: (l, 0, j)),
                  pl.BlockSpec((1, 1, tn), lambda l, j: (l, 0, j))],
        out_specs=pl.BlockSpec((1, nb, tn), lambda l, j: (l, 0, j)),
        out_shape=jax.ShapeDtypeStruct((nl, nb, n6), F32),
        compiler_params=_cparams("parallel", "parallel"),
        name="adaln_modulation",
    )(cb, w_mod, b_mod.reshape(nl, 1, n6))


IN_TN = 768
IN_TILES = (2, 3, 2, 8)


def _inproj_kernel(x_ref, sc_ref, sh_ref, w_ref, oa_ref, ob_ref, oc_ref, og_ref, h_scr):
    j = pl.program_id(1)

    @pl.when(j == 0)
    def _():
        h_scr[...] = (x_ref[...] * (1.0 + sc_ref[0]) + sh_ref[0]).astype(BF16)

    res = jnp.dot(h_scr[...], w_ref[...].astype(BF16), preferred_element_type=F32).astype(BF16)
    lo = 0
    for ref, n in zip((oa_ref, ob_ref, oc_ref, og_ref), IN_TILES):
        @pl.when((j >= lo) & (j < lo + n))
        def _(ref=ref):
            ref[...] = res
        lo += n


def _in_projection(x2d, mod3, w_all, layer, seq, tm=1024):
    t, d = x2d.shape
    tm = min(tm, seq)
    per_b = seq // tm
    starts = np.cumsum((0,) + IN_TILES[:-1]).tolist()

    def out_spec(lo, n):
        return pl.BlockSpec((tm, IN_TN), lambda i, j: (i, jnp.clip(j - lo, 0, n - 1)))

    return pl.pallas_call(
        _inproj_kernel,
        grid=(t // tm, sum(IN_TILES)),
        in_specs=[pl.BlockSpec((tm, d), lambda i, j: (i, 0)),
                  pl.BlockSpec((1, 1, d), lambda i, j: (i // per_b, 0, 1)),
                  pl.BlockSpec((1, 1, d), lambda i, j: (i // per_b, 0, 0)),
                  pl.BlockSpec((None, d, IN_TN), lambda i, j: (layer, 0, j))],
        out_specs=[out_spec(lo, n) for lo, n in zip(starts, IN_TILES)],
        out_shape=[jax.ShapeDtypeStruct((t, n * IN_TN), BF16) for n in IN_TILES],
        scratch_shapes=[pltpu.VMEM((tm, d), BF16)],
        compiler_params=_cparams("parallel", "arbitrary"),
        name="in_projection",
    )(x2d, mod3, mod3, w_all)


def _rope_tables(seq):
    rows = seq // GRID_W
    row = jnp.repeat(jnp.arange(rows, dtype=F32), GRID_W)
    col = jnp.tile(jnp.arange(GRID_W, dtype=F32), rows)
    quarter = HEAD_DIM // 4
    inv = ROPE_THETA ** (-jnp.arange(quarter, dtype=F32) / quarter)
    ar, ac = row[:, None] * inv, col[:, None] * inv
    ctab = jnp.concatenate([jnp.cos(ar), jnp.cos(ar), jnp.cos(ac), jnp.cos(ac)], axis=1)
    stab = jnp.concatenate([-jnp.sin(ar), jnp.sin(ar), -jnp.sin(ac), jnp.sin(ac)], axis=1)
    return ctab, stab


def _norm_rope(xf, gain, ctab, stab):
    hd = xf.shape[1]
    r = lax.broadcasted_iota(jnp.int32, (hd, hd), 0)
    c = lax.broadcasted_iota(jnp.int32, (hd, hd), 1)
    mean_mat = jnp.full((hd, hd), 1.0 / hd, BF16)
    swap_mat = jnp.where(r == (c ^ (hd // 4)), 1.0, 0.0).astype(BF16)
    ms = jnp.dot((xf * xf).astype(BF16), mean_mat, preferred_element_type=F32)
    xn = xf * lax.rsqrt(ms + QK_EPS) * gain
    partner = jnp.dot(xn.astype(BF16), swap_mat, preferred_element_type=F32)
    return xn * ctab + partner * stab


def _attn_a_kernel(q_ref, k_ref, v_ref, cq_ref, sq_ref, ck_ref, sk_ref, qg_ref, kg_ref,
                   o_ref, kp_scr, vp_scr, *, kc):
    tq, seq = q_ref.shape[0], k_ref.shape[0]
    grp = q_ref.shape[1] // HEAD_DIM

    @pl.when(pl.program_id(2) == 0)
    def _():
        kp_scr[...] = _norm_rope(k_ref[...].astype(F32), kg_ref[...], ck_ref[...], sk_ref[...]).astype(BF16)
        vp_scr[:, :HEAD_DIM] = v_ref[...]
        vp_scr[:, HEAD_DIM:] = jnp.ones((seq, HEAD_DIM), BF16)

    cq, sq, qg = cq_ref[...], sq_ref[...], qg_ref[...]
    scale = HEAD_DIM ** -0.5
    q4 = jnp.concatenate(
        [(_norm_rope(q_ref[:, h * HEAD_DIM:(h + 1) * HEAD_DIM].astype(F32), qg, cq, sq) * scale).astype(BF16)
         for h in range(grp)], axis=0)

    def scores(c):
        return lax.dot_general(q4, kp_scr[c * kc:(c + 1) * kc, :], (((1,), (1,)), ((), ())),
                               preferred_element_type=F32)

    nchunk = seq // kc
    s = scores(0)
    m = acc = None
    for c in range(nchunk):
        nxt = scores(c + 1) if c + 1 < nchunk else None
        smax = jnp.max(s, axis=-1, keepdims=True)
        mn = smax if m is None else jnp.maximum(m, smax)
        p = jnp.exp(s - mn).astype(BF16)
        pv = jnp.dot(p, vp_scr[c * kc:(c + 1) * kc, :], preferred_element_type=F32)
        acc = pv if m is None else jnp.exp(m - mn) * acc + pv
        m, s = mn, nxt
    o = acc[:, :HEAD_DIM] / acc[:, HEAD_DIM:HEAD_DIM + 1]
    for h in range(grp):
        o_ref[:, h * HEAD_DIM:(h + 1) * HEAD_DIM] = o[h * tq:(h + 1) * tq].astype(BF16)


def _mixer_a(qkv, ctab, stab, q_gain, k_gain, nb, seq, tq=256, kc=512):
    t = qkv.shape[0]
    nq = seq // tq
    gw = (A_Q_HEADS // A_KV_HEADS) * HEAD_DIM
    k_blk = A_Q_HEADS
    v_blk = A_Q_HEADS + A_KV_HEADS
    return pl.pallas_call(
        functools.partial(_attn_a_kernel, kc=kc),
        grid=(nb, A_KV_HEADS, nq),
        in_specs=[pl.BlockSpec((tq, gw), lambda b, g, i: (b * nq + i, g)),
                  pl.BlockSpec((seq, HEAD_DIM), lambda b, g, i: (b, k_blk + g)),
                  pl.BlockSpec((seq, HEAD_DIM), lambda b, g, i: (b, v_blk + g)),
                  pl.BlockSpec((tq, HEAD_DIM), lambda b, g, i: (i, 0)),
                  pl.BlockSpec((tq, HEAD_DIM), lambda b, g, i: (i, 0)),
                  pl.BlockSpec((seq, HEAD_DIM), lambda b, g, i: (0, 0)),
                  pl.BlockSpec((seq, HEAD_DIM), lambda b, g, i: (0, 0)),
                  pl.BlockSpec((1, HEAD_DIM), lambda b, g, i: (0, 0)),
                  pl.BlockSpec((1, HEAD_DIM), lambda b, g, i: (0, 0))],
        out_specs=pl.BlockSpec((tq, gw), lambda b, g, i: (b * nq + i, g)),
        out_shape=jax.ShapeDtypeStruct((t, A_Q_HEADS * HEAD_DIM), BF16),
        scratch_shapes=[pltpu.VMEM((seq, HEAD_DIM), BF16), pltpu.VMEM((seq, 2 * HEAD_DIM), BF16)],
        compiler_params=_cparams("parallel", "parallel", "arbitrary"),
        name="mixer_a_axial_gqa",
    )(qkv, qkv, qkv, ctab, stab, ctab, stab, q_gain.reshape(1, HEAD_DIM), k_gain.reshape(1, HEAD_DIM))


def _window(q0, tq, half, seq):
    span = min(tq + 2 * half, seq)
    start = jnp.clip(q0 - half, 0, seq - span)
    return pl.multiple_of(start, 64), span


def _attn_b_kernel(q_ref, k_ref, v_ref, o_ref):
    tq, seq = q_ref.shape[0], k_ref.shape[0]
    q0 = pl.program_id(1) * tq
    scale = HEAD_DIM ** -0.5
    outs, lses = [], []
    for g, (win, r) in enumerate(B_GROUPS):
        half = win // 2
        start, span = _window(q0, tq, half, seq)
        qpos = q0 + lax.broadcasted_iota(jnp.int32, (tq, span), 0)
        kpos = start + lax.broadcasted_iota(jnp.int32, (tq, span), 1)
        dd = qpos - kpos
        ad = jnp.abs(dd)
        valid = (ad <= half) & ((dd & (r - 1)) == 0)
        adf = ad.astype(F32)
        for hh in range(B_HEADS_PER_GROUP):
            head = g * B_HEADS_PER_GROUP + hh
            cols = slice(head * HEAD_DIM, (head + 1) * HEAD_DIM)
            slope = float(2.0 ** (-8.0 * (head + 1) / B_HEADS))
            s = lax.dot_general(q_ref[:, cols], k_ref[pl.ds(start, span), cols],
                                (((1,), (1,)), ((), ())), preferred_element_type=F32) * scale
            s = jnp.where(valid, s - slope * adf, NEG_INF)
            m = jnp.max(s, axis=-1, keepdims=True)
            p = jnp.exp(s - m)
            l = jnp.sum(p, axis=-1, keepdims=True)
            o = jnp.dot(p.astype(BF16), v_ref[pl.ds(start, span), cols], preferred_element_type=F32)
            outs.append(o / l)
            lses.append(m + jnp.log(l))
    ng = len(B_GROUPS)
    for hh in range(B_HEADS_PER_GROUP):
        hl = [lses[g * B_HEADS_PER_GROUP + hh] for g in range(ng)]
        mx = functools.reduce(jnp.maximum, hl)
        ex = [jnp.exp(x - mx) for x in hl]
        tot = functools.reduce(lambda a, b: a + b, ex)
        for g in range(ng):
            head = g * B_HEADS_PER_GROUP + hh
            o_ref[:, head * HEAD_DIM:(head + 1) * HEAD_DIM] = (outs[head] * (ex[g] / tot)).astype(BF16)


def _mixer_b(qkv, nb, seq, tq=256):
    t = qkv.shape[0]
    nq = seq // tq
    bw = B_HEADS * HEAD_DIM
    return pl.pallas_call(
        _attn_b_kernel,
        grid=(nb, nq),
        in_specs=[pl.BlockSpec((tq, bw), lambda b, i: (b * nq + i, 0)),
                  pl.BlockSpec((seq, bw), lambda b, i: (b, 1)),
                  pl.BlockSpec((seq, bw), lambda b, i: (b, 2))],
        out_specs=pl.BlockSpec((tq, bw), lambda b, i: (b * nq + i, 0)),
        out_shape=jax.ShapeDtypeStruct((t, bw), BF16),
        compiler_params=_cparams("parallel", "arbitrary"),
        name="mixer_b_dilated",
    )(qkv, qkv, qkv)


def _attn_c_kernel(hp_ref, q_ref, k_ref, v_ref, o_ref):
    tq, seq = q_ref.shape[0], k_ref.shape[0]
    grp = q_ref.shape[1] // HEAD_DIM
    g = pl.program_id(1)
    q0 = pl.program_id(2) * tq
    start, span = _window(q0, tq, C_HALF_WINDOW, seq)
    qpos = q0 + lax.broadcasted_iota(jnp.int32, (tq, span), 0)
    kpos = start + lax.broadcasted_iota(jnp.int32, (tq, span), 1)
    ad = jnp.abs(qpos - kpos)
    valid = ad <= C_HALF_WINDOW
    adf = ad.astype(F32)
    kk = k_ref[pl.ds(start, span), :]
    vv = v_ref[pl.ds(start, span), :]
    scale = HEAD_DIM ** -0.5
    for hh in range(grp):
        cols = slice(hh * HEAD_DIM, (hh + 1) * HEAD_DIM)
        slope, sink = hp_ref[g, hh], hp_ref[g, grp + hh]
        s = lax.dot_general(q_ref[:, cols], kk, (((1,), (1,)), ((), ())),
                            preferred_element_type=F32) * scale
        s = jnp.where(valid, s - slope * adf, NEG_INF)
        m = jnp.maximum(jnp.max(s, axis=-1, keepdims=True), sink)
        p = jnp.exp(s - m)
        l = jnp.sum(p, axis=-1, keepdims=True) + jnp.exp(sink - m)
        o = jnp.dot(p.astype(BF16), vv, preferred_element_type=F32)
        o_ref[:, cols] = (o / l).astype(BF16)


def _mixer_c(qkv, sink, nb, seq, tq=256):
    t = qkv.shape[0]
    nq = seq // tq
    grp = C_Q_HEADS // C_KV_HEADS
    gw = grp * HEAD_DIM
    slopes = (2.0 ** (-8.0 * jnp.arange(1, C_Q_HEADS + 1, dtype=F32) / C_Q_HEADS)).reshape(C_KV_HEADS, grp)
    head_params = jnp.concatenate([slopes, sink.astype(F32)], axis=1)
    return pl.pallas_call(
        _attn_c_kernel,
        grid=(nb, C_KV_HEADS, nq),
        in_specs=[pl.BlockSpec(memory_space=pltpu.SMEM),
                  pl.BlockSpec((tq, gw), lambda b, g, i: (b * nq + i, g)),
                  pl.BlockSpec((seq, HEAD_DIM), lambda b, g, i: (b, C_Q_HEADS + g)),
                  pl.BlockSpec((seq, HEAD_DIM), lambda b, g, i: (b, C_Q_HEADS + C_KV_HEADS + g))],
        out_specs=pl.BlockSpec((tq, gw), lambda b, g, i: (b * nq + i, g)),
        out_shape=jax.ShapeDtypeStruct((t, C_Q_HEADS * HEAD_DIM), BF16),
        compiler_params=_cparams("parallel", "parallel", "arbitrary"),
        name="mixer_c_window_sink",
    )(head_params, qkv, qkv, qkv)


def _layer_norm(z, g, b):
    mu = jnp.mean(z, axis=-1, keepdims=True)
    zc = z - mu
    var = jnp.mean(zc * zc, axis=-1, keepdims=True)
    return zc * lax.rsqrt(var + LN_EPS) * g + b


def _branch_kernel(oa_ref, ob_ref, oc_ref, gl_ref, wa_ref, wb_ref, wc_ref, m_ref):
    d = m_ref.shape[1]
    merged = None
    for i, (o_ref, w_ref) in enumerate(((oa_ref, wa_ref), (ob_ref, wb_ref), (oc_ref, wc_ref))):
        y = jnp.dot(o_ref[...], w_ref[...], preferred_element_type=F32)
        term = jax.nn.sigmoid(gl_ref[:, i * d:(i + 1) * d].astype(F32)) * y
        merged = term if merged is None else merged + term
    m_ref[...] = merged.astype(BF16)


def _outproj_kernel(m_ref, x_ref, ga_ref, scf_ref, shf_ref, lg_ref, lb_ref, wo_ref, x1_ref, ht_ref, *, alpha):
    y = jnp.dot(m_ref[...], wo_ref[...], preferred_element_type=F32)
    x1 = _layer_norm(alpha * x_ref[...] + (1.0 + ga_ref[0]) * y, lg_ref[...], lb_ref[...])
    x1_ref[...] = x1
    h2 = x1 * (1.0 + scf_ref[0]) + shf_ref[0]
    ht_ref[...] = h2.T.astype(BF16)


def _merge_project(oa, ob, oc, gl, x2d, mod3, ln_g, ln_b, wa, wb, wc, wo, seq, alpha, tm=512):
    t, d = x2d.shape
    per_b = seq // tm
    row = lambda w: pl.BlockSpec((tm, w), lambda i: (i, 0))
    whole = lambda a: pl.BlockSpec(a.shape, lambda i: (0, 0), pipeline_mode=pl.Buffered(1))
    modspec = lambda k: pl.BlockSpec((1, 1, d), lambda i: (i // per_b, 0, k))
    vec = pl.BlockSpec((1, d), lambda i: (0, 0))
    merged = pl.pallas_call(
        _branch_kernel,
        grid=(t // tm,),
        in_specs=[row(oa.shape[1]), row(ob.shape[1]), row(oc.shape[1]), row(gl.shape[1]),
                  whole(wa), whole(wb), whole(wc)],
        out_specs=row(d),
        out_shape=jax.ShapeDtypeStruct((t, d), BF16),
        compiler_params=_cparams("parallel"),
        name="branch_merge",
    )(oa, ob, oc, gl, wa, wb, wc)
    return pl.pallas_call(
        functools.partial(_outproj_kernel, alpha=alpha),
        grid=(t // tm,),
        in_specs=[row(d), row(d), modspec(2), modspec(4), modspec(3), vec, vec, whole(wo)],
        out_specs=[pl.BlockSpec((tm, d), lambda i: (i, 0)), pl.BlockSpec((d, tm), lambda i: (0, i))],
        out_shape=[jax.ShapeDtypeStruct((t, d), F32), jax.ShapeDtypeStruct((d, t), BF16)],
        compiler_params=_cparams("parallel"),
        name="outproj_ln",
    )(merged, x2d, mod3, mod3, mod3, ln_g.reshape(1, d), ln_b.reshape(1, d), wo)


CAND_ROWS = 4
ROUTE_GROUPS = 1


def _top_exact(s, n, order):
    nrow = lax.broadcasted_iota(jnp.int32, (n, s.shape[1]), 0)

    def step(i, carry):
        s, rank, vals = carry
        m = jnp.max(s, axis=0, keepdims=True)
        first = jnp.min(jnp.where(s == m, order, jnp.iinfo(jnp.int32).max), axis=0, keepdims=True)
        hit = order == first
        return jnp.where(hit, -jnp.inf, s), jnp.where(hit, i, rank), jnp.where(nrow == i, m, vals)

    _, rank, vals = lax.fori_loop(0, n, step, (s, jnp.full(s.shape, n, jnp.int32), jnp.zeros(nrow.shape, F32)))
    return vals, rank


def _top_fast(ss, n, want_rank):
    nrow = lax.broadcasted_iota(jnp.int32, (n, ss[0].shape[1]), 0)
    ss = list(ss)
    vals = [jnp.zeros(nrow.shape, F32) for _ in ss]
    ranks = [jnp.full(s.shape, n, jnp.int32) if w else None for s, w in zip(ss, want_rank)]
    for i in range(n):
        for a in range(len(ss)):
            m = jnp.max(ss[a], axis=0, keepdims=True)
            hit = ss[a] == m
            vals[a] = jnp.where(nrow == i, m, vals[a])
            if ranks[a] is not None:
                ranks[a] = jnp.where(hit, i, ranks[a])
            ss[a] = jnp.where(hit, -jnp.inf, ss[a])
    return vals, ranks


def _candidates(v1, v2):
    k, c = PEER_TOPK, CAND_ROWS
    ki = lax.broadcasted_iota(jnp.int32, v1.shape, 0)
    v1_tail = jnp.where(ki >= c, v1, -jnp.inf)
    cand = jnp.concatenate([v1[i:i + 1] + v2 for i in range(c)] + [v2[j:j + 1] + v1_tail for j in range(c)], axis=0)
    flat = jnp.concatenate([ki + i * k for i in range(c)] +
                           [jnp.where(ki >= c, ki * k + j, -1) for j in range(c)], axis=0)
    return cand, flat


def _route_outputs(s1, s2, v1, v2, rank2, sel, is_ith):
    k, c = PEER_TOPK, CAND_ROWS
    e1 = jnp.exp(v1 - v1[0:1])
    e2 = jnp.exp(v2 - v2[0:1])
    prod = jnp.concatenate([e1[i:i + 1] * e2 for i in range(c)] + [e2[j:j + 1] * e1 for j in range(c)], axis=0)
    z = jnp.sum(jnp.where(sel, prod, 0.0), axis=0, keepdims=True)
    self32 = sel.astype(F32)
    tail = functools.reduce(lambda a, b: a + b, [self32[(c + j) * k:(c + j + 1) * k] for j in range(c)])
    ja = jnp.zeros(s1.shape, F32)
    for i in range(k):
        cnt = jnp.sum(self32[i * k:(i + 1) * k], axis=0, keepdims=True) if i < c else tail[i:i + 1]
        ja = jnp.where(is_ith(i), cnt, ja)
    return (rank2.astype(F32).astype(BF16), jnp.exp(s2 - v2[0:1]).astype(BF16), ja,
            jnp.exp(s1 - v1[0:1]) / z)


def _route_exact(s1, s2):
    k = PEER_TOPK
    rows = lax.broadcasted_iota(jnp.int32, s1.shape, 0)
    v1, rank1 = _top_exact(s1, k, rows)
    v2, rank2 = _top_exact(s2, k, rows)
    cand, flat = _candidates(v1, v2)
    _, crank = _top_exact(cand, k, flat)
    return _route_outputs(s1, s2, v1, v2, rank2, crank < k, lambda i: rank1 == i)


def _route_fast(s1s, s2s):
    k, n = PEER_TOPK, len(s1s)
    vals, ranks = _top_fast(list(s1s) + list(s2s), k, [False] * n + [True] * n)
    cands = [_candidates(vals[g], vals[n + g])[0] for g in range(n)]
    cvals, _ = _top_fast(cands, k, [False] * n)
    outs = []
    for g in range(n):
        s1, s2, v1, v2 = s1s[g], s2s[g], vals[g], vals[n + g]
        sel = cands[g] >= cvals[g][k - 1:k]

        def count(mask):
            return jnp.sum(mask.astype(jnp.int32), axis=0, keepdims=True)

        tie = (count(s1 >= v1[k - 1:k]) != k) | (count(ranks[n + g] < k) != k) | (count(sel) != k)
        res = _route_outputs(s1, s2, v1, v2, ranks[n + g], sel, lambda i, s1=s1, v1=v1: s1 == v1[i:i + 1])
        outs.append((res, tie.astype(jnp.int32)))
    return outs


def _route_kernel(ht_ref, wqt_ref, keys_ref, r2_ref, e2_ref, ja_ref, cf_ref, qt_scr):
    nk = PEER_N_KEYS
    tm = ht_ref.shape[1]
    qt_scr[...] = jnp.dot(wqt_ref[...], ht_ref[...], preferred_element_type=F32).astype(BF16)

    def head(h, carry):
        q0 = pl.multiple_of(h * 2 * nk, 2 * nk)
        s1 = jnp.dot(keys_ref[0], qt_scr[pl.ds(q0, nk), :], preferred_element_type=F32)
        s2 = jnp.dot(keys_ref[1], qt_scr[pl.ds(q0 + nk, nk), :], preferred_element_type=F32)
        def store(cols, res):
            r2_ref[h, :, cols], e2_ref[h, :, cols], ja_ref[h, :, cols], cf_ref[h, :, cols] = res

        for g0 in range(0, tm // LANES, ROUTE_GROUPS):
            colss = [slice(g * LANES, (g + 1) * LANES) for g in range(g0, g0 + ROUTE_GROUPS)]
            outs = _route_fast([s1[:, c] for c in colss], [s2[:, c] for c in colss])
            for cols, (res, tie) in zip(colss, outs):
                store(cols, res)

                @pl.when(jnp.max(tie) > 0)
                def _(cols=cols):
                    store(cols, _route_exact(s1[:, cols], s2[:, cols]))
        return carry

    lax.fori_loop(0, PEER_HEADS, head, 0)


def _peer_route(ht, wqt, keys_bf, tm=512):
    d, t = ht.shape
    shp = lambda dt: jax.ShapeDtypeStruct((PEER_HEADS, PEER_N_KEYS, t), dt)
    spec = pl.BlockSpec((PEER_HEADS, PEER_N_KEYS, tm), lambda i: (0, 0, i))
    return pl.pallas_call(
        _route_kernel,
        grid=(t // tm,),
        in_specs=[pl.BlockSpec((d, tm), lambda i: (0, i)),
                  pl.BlockSpec(wqt.shape, lambda i: (0, 0)),
                  pl.BlockSpec(keys_bf.shape, lambda i: (0, 0, 0))],
        out_specs=[spec] * 4,
        out_shape=[shp(BF16), shp(BF16), shp(F32), shp(F32)],
        scratch_shapes=[pltpu.VMEM((wqt.shape[0], tm), BF16)],
        compiler_params=_cparams("parallel"),
        name="peer_route",
    )(ht, wqt, keys_bf)


EXP_CHUNK = 512

def _experts_kernel(ht_ref, u_ref, vt_ref, r2_ref, e2_ref, ja_ref, cf_ref, yt_ref):
    j = pl.program_id(1)
    nk = PEER_N_KEYS
    te = u_ref.shape[0]

    @pl.when(j == 0)
    def _():
        yt_ref[...] = jnp.zeros_like(yt_ref)

    zero = jnp.zeros((), BF16)
    ht = ht_ref[...]
    nchunk = te // EXP_CHUNK

    def hidden(c):
        return jnp.dot(u_ref[c * EXP_CHUNK:(c + 1) * EXP_CHUNK, :], ht, preferred_element_type=F32)

    act = hidden(0)
    for c in range(nchunk):
        nxt = hidden(c + 1) if c + 1 < nchunk else None
        gel = (0.5 * act * (1.0 + lax.erf(act * (2.0 ** -0.5)))).astype(BF16)
        parts = []
        for aa in range(EXP_CHUNK // nk):
            a = j * (te // nk) + c * (EXP_CHUNK // nk) + aa
            gate = None
            for h in range(PEER_HEADS):
                ja = ja_ref[h, pl.ds(a, 1), :].astype(BF16)
                cf = cf_ref[h, pl.ds(a, 1), :].astype(BF16)
                term = jnp.where(r2_ref[h] < ja, e2_ref[h] * cf, zero)
                gate = term if gate is None else gate + term
            parts.append(gate * gel[aa * nk:(aa + 1) * nk])
        wt = jnp.concatenate(parts, axis=0)
        yt_ref[...] += jnp.dot(vt_ref[:, c * EXP_CHUNK:(c + 1) * EXP_CHUNK], wt, preferred_element_type=F32)
        act = nxt


def _peer_experts(ht, u_bf, vt_bf, r2, e2, ja, cf, tm=1024, te=1024):
    d, t = ht.shape
    tm = min(tm, t)
    ne = u_bf.shape[0]
    once = pl.Buffered(1)
    side = pl.BlockSpec((PEER_HEADS, PEER_N_KEYS, tm), lambda i, j: (0, 0, i), pipeline_mode=once)
    return pl.pallas_call(
        _experts_kernel,
        grid=(t // tm, ne // te),
        in_specs=[pl.BlockSpec((d, tm), lambda i, j: (0, i), pipeline_mode=once),
                  pl.BlockSpec((te, d), lambda i, j: (j, 0)),
                  pl.BlockSpec((d, te), lambda i, j: (0, j)),
                  side, side, side, side],
        out_specs=pl.BlockSpec((d, tm), lambda i, j: (0, i)),
        out_shape=jax.ShapeDtypeStruct((d, t), F32),
        compiler_params=_cparams("parallel", "arbitrary"),
        name="peer_experts",
    )(ht, u_bf, vt_bf, r2, e2, ja, cf)


def _resid_ln_kernel(x_ref, yt_ref, gf_ref, lg_ref, lb_ref, o_ref, *, alpha):
    o_ref[...] = _layer_norm(alpha * x_ref[...] + (1.0 + gf_ref[0]) * yt_ref[...].T, lg_ref[...], lb_ref[...])


def _resid_ln(x2d, yt, mod3, ln_g, ln_b, seq, alpha, tm=512):
    t, d = x2d.shape
    per_b = seq // tm
    row = pl.BlockSpec((tm, d), lambda i: (i, 0))
    vec = pl.BlockSpec((1, d), lambda i: (0, 0))
    return pl.pallas_call(
        functools.partial(_resid_ln_kernel, alpha=alpha),
        grid=(t // tm,),
        in_specs=[row, pl.BlockSpec((d, tm), lambda i: (0, i)),
                  pl.BlockSpec((1, 1, d), lambda i: (i // per_b, 0, 5)), vec, vec],
        out_specs=row,
        out_shape=jax.ShapeDtypeStruct((t, d), F32),
        compiler_params=_cparams("parallel"),
        name="resid_ln",
    )(x2d, yt, mod3, ln_g.reshape(1, d), ln_b.reshape(1, d))


def kernel(x, c, w_mod, b_mod, w_in, a_q_gain, a_k_gain, c_sink, w_pa, w_pb, w_pc, w_o,
           ln1_g, ln1_b, peer_wq, peer_keys, peer_u, peer_v, ln2_g, ln2_b):
    nb, seq, d = x.shape
    depth = w_mod.shape[0]
    alpha = float((2 * depth) ** 0.25)
    ctab, stab = _rope_tables(seq)
    mod = _modulation(c, w_mod, b_mod)
    x2d = x.reshape(nb * seq, d)
    for l in range(depth):
        mod3 = mod[l].reshape(nb, 1, 6 * d)
        qkv_a, qkv_b, qkv_c, gl = _in_projection(x2d, mod3, w_in, l, seq)
        oa = _mixer_a(qkv_a, ctab, stab, a_q_gain[l], a_k_gain[l], nb, seq)
        ob = _mixer_b(qkv_b, nb, seq)
        oc = _mixer_c(qkv_c, c_sink[l], nb, seq)
        x1, ht = _merge_project(oa, ob, oc, gl, x2d, mod3, ln1_g[l], ln1_b[l],
                                w_pa[l].astype(BF16), w_pb[l].astype(BF16), w_pc[l].astype(BF16),
                                w_o[l].astype(BF16), seq, alpha)
        r2, e2, ja, cf = _peer_route(ht, peer_wq[l].T.astype(BF16), peer_keys[l].astype(BF16))
        yt = _peer_experts(ht, peer_u[l].astype(BF16), peer_v[l].T.astype(BF16), r2, e2, ja, cf)
        x2d = _resid_ln(x1, yt, mod3, ln2_g[l], ln2_b[l], seq, alpha)
    return x2d.reshape(nb, seq, d)
```

```python
import functools

import numpy as np
import jax
import jax.numpy as jnp
from jax import lax
from jax.experimental import pallas as pl
from jax.experimental.pallas import tpu as pltpu

F32 = jnp.float32
BF16 = jnp.bfloat16

HEAD_DIM = 128
A_Q_HEADS, A_KV_HEADS = 8, 2
B_GROUPS = ((128, 1), (512, 4), (2048, 16))
B_HEADS_PER_GROUP = 2
B_HEADS = B_HEADS_PER_GROUP * len(B_GROUPS)
C_Q_HEADS, C_KV_HEADS = 8, 2
C_HALF_WINDOW = 128
N_BRANCHES = 3
GRID_W = 64
ROPE_THETA = 10000.0
PEER_HEADS = 8
PEER_N_KEYS = 128
PEER_TOPK = 16
LN_EPS = 1e-5
QK_EPS = 1e-6
NEG_INF = -1e30

LANES = 128
SUBLANES = 8
BF16_ROWS = 16
VMEM_LIMIT = 56 * 1024 * 1024
MERGE_VMEM_LIMIT = 58 * 1024 * 1024


def _cparams(*sem, vmem=VMEM_LIMIT):
    return pltpu.CompilerParams(dimension_semantics=sem, vmem_limit_bytes=vmem)


def _mod_kernel(cb_ref, w_ref, b_ref, o_ref):
    nb, tn = cb_ref.shape[0], w_ref.shape[2]
    rows = []
    for b in range(nb):
        cb = cb_ref[b]
        pieces = [jnp.sum(w_ref[0, :, n * LANES:(n + 1) * LANES] * cb, axis=0, keepdims=True)
                  for n in range(tn // LANES)]
        rows.append(jnp.concatenate(pieces, axis=1))
    o_ref[0] = jnp.concatenate(rows, axis=0) + b_ref[0]


def _modulation(c, w_mod, b_mod, tn=1024):
    nl, d, n6 = w_mod.shape
    nb = c.shape[0]
    cb = jnp.broadcast_to(c[:, :, None], (nb, d, LANES))
    return pl.pallas_call(
        _mod_kernel,
        grid=(nl, n6 // tn),
        in_specs=[pl.BlockSpec((nb, d, LANES), lambda l, j: (0, 0, 0)),
                  pl.BlockSpec((1, d, tn), lambda l, j: (l, 0, j)),
                  pl.BlockSpec((1, 1, tn), lambda l, j: (l, 0, j))],
        out_specs=pl.BlockSpec((1, nb, tn), lambda l, j: (l, 0, j)),
        out_shape=jax.ShapeDtypeStruct((nl, nb, n6), F32),
        compiler_params=_cparams("parallel", "parallel"),
        name="adaln_modulation",
    )(cb, w_mod, b_mod.reshape(nl, 1, n6))


IN_TN = 768
IN_TILES = (2, 3, 2, 8)


def _inproj_kernel(x_ref, sc_ref, sh_ref, w_ref, oa_ref, ob_ref, oc_ref, og_ref, h_scr):
    j = pl.program_id(1)

    @pl.when(j == 0)
    def _():
        h_scr[...] = (x_ref[...] * (1.0 + sc_ref[0]) + sh_ref[0]).astype(BF16)

    res = jnp.dot(h_scr[...], w_ref[...].astype(BF16), preferred_element_type=F32).astype(BF16)
    lo = 0
    for ref, n in zip((oa_ref, ob_ref, oc_ref, og_ref), IN_TILES):
        @pl.when((j >= lo) & (j < lo + n))
        def _(ref=ref):
            ref[...] = res
        lo += n


def _in_projection(x2d, mod3, w_all, layer, seq, tm=1024):
    t, d = x2d.shape
    tm = min(tm, seq)
    per_b = seq // tm
    starts = np.cumsum((0,) + IN_TILES[:-1]).tolist()

    def out_spec(lo, n):
        return pl.BlockSpec((tm, IN_TN), lambda i, j: (i, jnp.clip(j - lo, 0, n - 1)))

    return pl.pallas_call(
        _inproj_kernel,
        grid=(t // tm, sum(IN_TILES)),
        in_specs=[pl.BlockSpec((tm, d), lambda i, j: (i, 0)),
                  pl.BlockSpec((1, 1, d), lambda i, j: (i // per_b, 0, 1)),
                  pl.BlockSpec((1, 1, d), lambda i, j: (i // per_b, 0, 0)),
                  pl.BlockSpec((None, d, IN_TN), lambda i, j: (layer, 0, j))],
        out_specs=[out_spec(lo, n) for lo, n in zip(starts, IN_TILES)],
        out_shape=[jax.ShapeDtypeStruct((t, n * IN_TN), BF16) for n in IN_TILES],
        scratch_shapes=[pltpu.VMEM((tm, d), BF16)],
        compiler_params=_cparams("parallel", "arbitrary"),
        name="in_projection",
    )(x2d, mod3, mod3, w_all)


def _rope_tables(seq):
    rows = seq // GRID_W
    row = jnp.repeat(jnp.arange(rows, dtype=F32), GRID_W)
    col = jnp.tile(jnp.arange(GRID_W, dtype=F32), rows)
    quarter = HEAD_DIM // 4
    inv = ROPE_THETA ** (-jnp.arange(quarter, dtype=F32) / quarter)
    ar, ac = row[:, None] * inv, col[:, None] * inv
    ctab = jnp.concatenate([jnp.cos(ar), jnp.cos(ar), jnp.cos(ac), jnp.cos(ac)], axis=1)
    stab = jnp.concatenate([-jnp.sin(ar), jnp.sin(ar), -jnp.sin(ac), jnp.sin(ac)], axis=1)
    return ctab, stab


def _norm_rope(xf, gain, ctab, stab):
    hd = xf.shape[1]
    r = lax.broadcasted_iota(jnp.int32, (hd, hd), 0)
    c = lax.broadcasted_iota(jnp.int32, (hd, hd), 1)
    mean_mat = jnp.full((hd, hd), 1.0 / hd, BF16)
    swap_mat = jnp.where(r == (c ^ (hd // 4)), 1.0, 0.0).astype(BF16)
    ms = jnp.dot((xf * xf).astype(BF16), mean_mat, preferred_element_type=F32)
    xn = xf * lax.rsqrt(ms + QK_EPS) * gain
    partner = jnp.dot(xn.astype(BF16), swap_mat, preferred_element_type=F32)
    return xn * ctab + partner * stab


def _attn_a_kernel(q_ref, k_ref, v_ref, cq_ref, sq_ref, ck_ref, sk_ref, qg_ref, kg_ref,
                   o_ref, kp_scr, vp_scr, *, kc):
    tq, seq = q_ref.shape[0], k_ref.shape[0]
    grp = q_ref.shape[1] // HEAD_DIM

    @pl.when(pl.program_id(2) == 0)
    def _():
        kp_scr[...] = _norm_rope(k_ref[...].astype(F32), kg_ref[...], ck_ref[...], sk_ref[...]).astype(BF16)
        vp_scr[:, :HEAD_DIM] = v_ref[...]
        vp_scr[:, HEAD_DIM:] = jnp.ones((seq, HEAD_DIM), BF16)

    cq, sq, qg = cq_ref[...], sq_ref[...], qg_ref[...]
    scale = HEAD_DIM ** -0.5
    q4 = jnp.concatenate(
        [(_norm_rope(q_ref[:, h * HEAD_DIM:(h + 1) * HEAD_DIM].astype(F32), qg, cq, sq) * scale).astype(BF16)
         for h in range(grp)], axis=0)

    def scores(c):
        return lax.dot_general(q4, kp_scr[c * kc:(c + 1) * kc, :], (((1,), (1,)), ((), ())),
                               preferred_element_type=F32)

    nchunk = seq // kc
    s = scores(0)
    m = acc = None
    for c in range(nchunk):
        nxt = scores(c + 1) if c + 1 < nchunk else None
        smax = jnp.max(s, axis=-1, keepdims=True)
        mn = smax if m is None else jnp.maximum(m, smax)
        p = jnp.exp(s - mn).astype(BF16)
        pv = jnp.dot(p, vp_scr[c * kc:(c + 1) * kc, :], preferred_element_type=F32)
        acc = pv if m is None else jnp.exp(m - mn) * acc + pv
        m, s = mn, nxt
    o = acc[:, :HEAD_DIM] / acc[:, HEAD_DIM:HEAD_DIM + 1]
    for h in range(grp):
        o_ref[:, h * HEAD_DIM:(h + 1) * HEAD_DIM] = o[h * tq:(h + 1) * tq].astype(BF16)


def _mixer_a(qkv, ctab, stab, q_gain, k_gain, nb, seq, tq=256, kc=512):
    t = qkv.shape[0]
    nq = seq // tq
    gw = (A_Q_HEADS // A_KV_HEADS) * HEAD_DIM
    k_blk = A_Q_HEADS
    v_blk = A_Q_HEADS + A_KV_HEADS
    return pl.pallas_call(
        functools.partial(_attn_a_kernel, kc=kc),
        grid=(nb, A_KV_HEADS, nq),
        in_specs=[pl.BlockSpec((tq, gw), lambda b, g, i: (b * nq + i, g)),
                  pl.BlockSpec((seq, HEAD_DIM), lambda b, g, i: (b, k_blk + g)),
                  pl.BlockSpec((seq, HEAD_DIM), lambda b, g, i: (b, v_blk + g)),
                  pl.BlockSpec((tq, HEAD_DIM), lambda b, g, i: (i, 0)),
                  pl.BlockSpec((tq, HEAD_DIM), lambda b, g, i: (i, 0)),
                  pl.BlockSpec((seq, HEAD_DIM), lambda b, g, i: (0, 0)),
                  pl.BlockSpec((seq, HEAD_DIM), lambda b, g, i: (0, 0)),
                  pl.BlockSpec((1, HEAD_DIM), lambda b, g, i: (0, 0)),
                  pl.BlockSpec((1, HEAD_DIM), lambda b, g, i: (0, 0))],
        out_specs=pl.BlockSpec((tq, gw), lambda b, g, i: (b * nq + i, g)),
        out_shape=jax.ShapeDtypeStruct((t, A_Q_HEADS * HEAD_DIM), BF16),
        scratch_shapes=[pltpu.VMEM((seq, HEAD_DIM), BF16), pltpu.VMEM((seq, 2 * HEAD_DIM), BF16)],
        compiler_params=_cparams("parallel", "parallel", "arbitrary"),
        name="mixer_a_axial_gqa",
    )(qkv, qkv, qkv, ctab, stab, ctab, stab, q_gain.reshape(1, HEAD_DIM), k_gain.reshape(1, HEAD_DIM))


def _window(q0, tq, half, seq):
    span = min(tq + 2 * half, seq)
    start = jnp.clip(q0 - half, 0, seq - span)
    return pl.multiple_of(start, 64), span


def _attn_b_kernel(q_ref, k_ref, v_ref, o_ref):
    tq, seq = q_ref.shape[0], k_ref.shape[0]
    q0 = pl.program_id(1) * tq
    scale = HEAD_DIM ** -0.5
    outs, lses = [], []
    for g, (win, r) in enumerate(B_GROUPS):
        half = win // 2
        start, span = _window(q0, tq, half, seq)
        qpos = q0 + lax.broadcasted_iota(jnp.int32, (tq, span), 0)
        kpos = start + lax.broadcasted_iota(jnp.int32, (tq, span), 1)
        dd = qpos - kpos
        ad = jnp.abs(dd)
        valid = (ad <= half) & ((dd & (r - 1)) == 0)
        adf = ad.astype(F32)
        for hh in range(B_HEADS_PER_GROUP):
            head = g * B_HEADS_PER_GROUP + hh
            cols = slice(head * HEAD_DIM, (head + 1) * HEAD_DIM)
            slope = float(2.0 ** (-8.0 * (head + 1) / B_HEADS))
            s = lax.dot_general(q_ref[:, cols], k_ref[pl.ds(start, span), cols],
                                (((1,), (1,)), ((), ())), preferred_element_type=F32) * scale
            s = jnp.where(valid, s - slope * adf, NEG_INF)
            m = jnp.max(s, axis=-1, keepdims=True)
            p = jnp.exp(s - m)
            l = jnp.sum(p, axis=-1, keepdims=True)
            o = jnp.dot(p.astype(BF16), v_ref[pl.ds(start, span), cols], preferred_element_type=F32)
            outs.append(o / l)
            lses.append(m + jnp.log(l))
    ng = len(B_GROUPS)
    for hh in range(B_HEADS_PER_GROUP):
        hl = [lses[g * B_HEADS_PER_GROUP + hh] for g in range(ng)]
        mx = functools.reduce(jnp.maximum, hl)
        ex = [jnp.exp(x - mx) for x in hl]
        tot = functools.reduce(lambda a, b: a + b, ex)
        for g in range(ng):
            head = g * B_HEADS_PER_GROUP + hh
            o_ref[:, head * HEAD_DIM:(head + 1) * HEAD_DIM] = (outs[head] * (ex[g] / tot)).astype(BF16)


def _mixer_b(qkv, nb, seq, tq=256):
    t = qkv.shape[0]
    nq = seq // tq
    bw = B_HEADS * HEAD_DIM
    return pl.pallas_call(
        _attn_b_kernel,
        grid=(nb, nq),
        in_specs=[pl.BlockSpec((tq, bw), lambda b, i: (b * nq + i, 0)),
                  pl.BlockSpec((seq, bw), lambda b, i: (b, 1)),
                  pl.BlockSpec((seq, bw), lambda b, i: (b, 2))],
        out_specs=pl.BlockSpec((tq, bw), lambda b, i: (b * nq + i, 0)),
        out_shape=jax.ShapeDtypeStruct((t, bw), BF16),
        compiler_params=_cparams("parallel", "arbitrary"),
        name="mixer_b_dilated",
    )(qkv, qkv, qkv)


def _attn_c_kernel(hp_ref, q_ref, k_ref, v_ref, o_ref):
    tq, seq = q_ref.shape[0], k_ref.shape[0]
    grp = q_ref.shape[1] // HEAD_DIM
    g = pl.program_id(1)
    q0 = pl.program_id(2) * tq
    start, span = _window(q0, tq, C_HALF_WINDOW, seq)
    qpos = q0 + lax.broadcasted_iota(jnp.int32, (tq, span), 0)
    kpos = start + lax.broadcasted_iota(jnp.int32, (tq, span), 1)
    ad = jnp.abs(qpos - kpos)
    valid = ad <= C_HALF_WINDOW
    adf = ad.astype(F32)
    kk = k_ref[pl.ds(start, span), :]
    vv = v_ref[pl.ds(start, span), :]
    scale = HEAD_DIM ** -0.5
    for hh in range(grp):
        cols = slice(hh * HEAD_DIM, (hh + 1) * HEAD_DIM)
        slope, sink = hp_ref[g, hh], hp_ref[g, grp + hh]
        s = lax.dot_general(q_ref[:, cols], kk, (((1,), (1,)), ((), ())),
                            preferred_element_type=F32) * scale
        s = jnp.where(valid, s - slope * adf, NEG_INF)
        m = jnp.maximum(jnp.max(s, axis=-1, keepdims=True), sink)
        p = jnp.exp(s - m)
        l = jnp.sum(p, axis=-1, keepdims=True) + jnp.exp(sink - m)
        o = jnp.dot(p.astype(BF16), vv, preferred_element_type=F32)
        o_ref[:, cols] = (o / l).astype(BF16)


def _mixer_c(qkv, sink, nb, seq, tq=256):
    t = qkv.shape[0]
    nq = seq // tq
    grp = C_Q_HEADS // C_KV_HEADS
    gw = grp * HEAD_DIM
    slopes = (2.0 ** (-8.0 * jnp.arange(1, C_Q_HEADS + 1, dtype=F32) / C_Q_HEADS)).reshape(C_KV_HEADS, grp)
    head_params = jnp.concatenate([slopes, sink.astype(F32)], axis=1)
    return pl.pallas_call(
        _attn_c_kernel,
        grid=(nb, C_KV_HEADS, nq),
        in_specs=[pl.BlockSpec(memory_space=pltpu.SMEM),
                  pl.BlockSpec((tq, gw), lambda b, g, i: (b * nq + i, g)),
                  pl.BlockSpec((seq, HEAD_DIM), lambda b, g, i: (b, C_Q_HEADS + g)),
                  pl.BlockSpec((seq, HEAD_DIM), lambda b, g, i: (b, C_Q_HEADS + C_KV_HEADS + g))],
        out_specs=pl.BlockSpec((tq, gw), lambda b, g, i: (b * nq + i, g)),
        out_shape=jax.ShapeDtypeStruct((t, C_Q_HEADS * HEAD_DIM), BF16),
        compiler_params=_cparams("parallel", "parallel", "arbitrary"),
        name="mixer_c_window_sink",
    )(head_params, qkv, qkv, qkv)


def _layer_norm(z, g, b):
    mu = jnp.mean(z, axis=-1, keepdims=True)
    zc = z - mu
    var = jnp.mean(zc * zc, axis=-1, keepdims=True)
    return zc * lax.rsqrt(var + LN_EPS) * g + b


def _branch_kernel(oa_ref, ob_ref, oc_ref, gl_ref, wa_ref, wb_ref, wc_ref, m_ref):
    d = m_ref.shape[1]
    merged = None
    for i, (o_ref, w_ref) in enumerate(((oa_ref, wa_ref), (ob_ref, wb_ref), (oc_ref, wc_ref))):
        y = jnp.dot(o_ref[...], w_ref[...], preferred_element_type=F32)
        term = jax.nn.sigmoid(gl_ref[:, i * d:(i + 1) * d].astype(F32)) * y
        merged = term if merged is None else merged + term
    m_ref[...] = merged.astype(BF16)


def _outproj_kernel(m_ref, x_ref, ga_ref, scf_ref, shf_ref, lg_ref, lb_ref, wo_ref, x1_ref, ht_ref, *, alpha):
    y = jnp.dot(m_ref[...], wo_ref[...], preferred_element_type=F32)
    x1 = _layer_norm(alpha * x_ref[...] + (1.0 + ga_ref[0]) * y, lg_ref[...], lb_ref[...])
    x1_ref[...] = x1
    h2 = x1 * (1.0 + scf_ref[0]) + shf_ref[0]
    ht_ref[...] = h2.T.astype(BF16)


def _merge_project(oa, ob, oc, gl, x2d, mod3, ln_g, ln_b, wa, wb, wc, wo, seq, alpha, tm=512):
    t, d = x2d.shape
    per_b = seq // tm
    row = lambda w: pl.BlockSpec((tm, w), lambda i: (i, 0))
    whole = lambda a: pl.BlockSpec(a.shape, lambda i: (0, 0), pipeline_mode=pl.Buffered(1))
    modspec = lambda k: pl.BlockSpec((1, 1, d), lambda i: (i // per_b, 0, k))
    vec = pl.BlockSpec((1, d), lambda i: (0, 0))
    merged = pl.pallas_call(
        _branch_kernel,
        grid=(t // tm,),
        in_specs=[row(oa.shape[1]), row(ob.shape[1]), row(oc.shape[1]), row(gl.shape[1]),
                  whole(wa), whole(wb), whole(wc)],
        out_specs=row(d),
        out_shape=jax.ShapeDtypeStruct((t, d), BF16),
        compiler_params=_cparams("parallel"),
        name="branch_merge",
    )(oa, ob, oc, gl, wa, wb, wc)
    return pl.pallas_call(
        functools.partial(_outproj_kernel, alpha=alpha),
        grid=(t // tm,),
        in_specs=[row(d), row(d), modspec(2), modspec(4), modspec(3), vec, vec, whole(wo)],
        out_specs=[pl.BlockSpec((tm, d), lambda i: (i, 0)), pl.BlockSpec((d, tm), lambda i: (0, i))],
        out_shape=[jax.ShapeDtypeStruct((t, d), F32), jax.ShapeDtypeStruct((d, t), BF16)],
        compiler_params=_cparams("parallel"),
        name="outproj_ln",
    )(merged, x2d, mod3, mod3, mod3, ln_g.reshape(1, d), ln_b.reshape(1, d), wo)


CAND_ROWS = 4


def _top_exact(s, n, order):
    nrow = lax.broadcasted_iota(jnp.int32, (n, s.shape[1]), 0)

    def step(i, carry):
        s, rank, vals = carry
        m = jnp.max(s, axis=0, keepdims=True)
        first = jnp.min(jnp.where(s == m, order, jnp.iinfo(jnp.int32).max), axis=0, keepdims=True)
        hit = order == first
        return jnp.where(hit, -jnp.inf, s), jnp.where(hit, i, rank), jnp.where(nrow == i, m, vals)

    _, rank, vals = lax.fori_loop(0, n, step, (s, jnp.full(s.shape, n, jnp.int32), jnp.zeros(nrow.shape, F32)))
    return vals, rank


def _candidates(v1, v2):
    k, c = PEER_TOPK, CAND_ROWS
    ki = lax.broadcasted_iota(jnp.int32, v1.shape, 0)
    v1_tail = jnp.where(ki >= c, v1, -jnp.inf)
    cand = jnp.concatenate([v1[i:i + 1] + v2 for i in range(c)] + [v2[j:j + 1] + v1_tail for j in range(c)], axis=0)
    flat = jnp.concatenate([ki + i * k for i in range(c)] +
                           [jnp.where(ki >= c, ki * k + j, -1) for j in range(c)], axis=0)
    return cand, flat


def _route_outputs(s1, s2, v1, v2, rank2, sel, is_ith):
    k, c = PEER_TOPK, CAND_ROWS
    e1 = jnp.exp(v1 - v1[0:1])
    e2 = jnp.exp(v2 - v2[0:1])
    prod = jnp.concatenate([e1[i:i + 1] * e2 for i in range(c)] + [e2[j:j + 1] * e1 for j in range(c)], axis=0)
    z = jnp.sum(jnp.where(sel, prod, 0.0), axis=0, keepdims=True)
    self32 = sel.astype(F32)
    tail = functools.reduce(lambda a, b: a + b, [self32[(c + j) * k:(c + j + 1) * k] for j in range(c)])
    ja = jnp.zeros(s1.shape, F32)
    for i in range(k):
        cnt = jnp.sum(self32[i * k:(i + 1) * k], axis=0, keepdims=True) if i < c else tail[i:i + 1]
        ja = jnp.where(is_ith(i), cnt, ja)
    return (rank2.astype(F32).astype(BF16), jnp.exp(s2 - v2[0:1]).astype(BF16), ja,
            jnp.exp(s1 - v1[0:1]) / z)


def _route_exact(s1, s2):
    k = PEER_TOPK
    rows = lax.broadcasted_iota(jnp.int32, s1.shape, 0)
    v1, rank1 = _top_exact(s1, k, rows)
    v2, rank2 = _top_exact(s2, k, rows)
    cand, flat = _candidates(v1, v2)
    _, crank = _top_exact(cand, k, flat)
    return _route_outputs(s1, s2, v1, v2, rank2, crank < k, lambda i: rank1 == i)


def _sort_network(n):
    def merge(lo, hi, r):
        step = r * 2
        if step < hi - lo:
            yield from merge(lo, hi, step)
            yield from merge(lo + r, hi, step)
            yield from [(i, i + r) for i in range(lo + r, hi - r, step)]
        else:
            yield (lo, lo + r)

    def sort(lo, hi):
        if hi - lo >= 1:
            mid = lo + (hi - lo) // 2
            yield from sort(lo, mid)
            yield from sort(mid + 1, hi)
            yield from merge(lo, hi, 1)

    return tuple(sort(0, n - 1))


def _rep_max(x):
    return jnp.broadcast_to(jnp.max(x, axis=0, keepdims=True), x.shape)


def _rep_sum(x):
    return jnp.broadcast_to(jnp.sum(x, axis=0, keepdims=True), x.shape)


def _top_sorted(groups, n):
    work = [list(g) for g in groups]
    for i, j in _sort_network(len(work[0])):
        for v in work:
            v[i], v[j] = jnp.maximum(v[i], v[j]), jnp.minimum(v[i], v[j])
    vals = [[] for _ in work]
    for it in range(n):
        for v, out in zip(work, vals):
            m = _rep_max(v[0])
            out.append(m)
            hit = v[0] == m
            for k in range(n - 1 - it):
                v[k] = jnp.where(hit, v[k + 1], v[k])
    return vals


def _route_sorted(s1, s2):
    k, c, sub = PEER_TOPK, CAND_ROWS, SUBLANES
    nblk = s1.shape[0] // sub
    b1 = [s1[sub * q:sub * (q + 1)] for q in range(nblk)]
    b2 = [s2[sub * q:sub * (q + 1)] for q in range(nblk)]
    v1, v2 = _top_sorted([b1, b2], k)
    srow = lax.broadcasted_iota(jnp.int32, b1[0].shape, 0)

    def stack(vals):
        return [functools.reduce(lambda acc, t: jnp.where(srow == t, vals[blk * sub + t], acc),
                                 range(1, sub), vals[blk * sub]) for blk in range(k // sub)]

    st1, st2 = stack(v1), stack(v2)
    st1_tail = [jnp.where(srow >= c, st1[0], -jnp.inf)] + st1[1:]
    cand = ([v1[i] + st2[b] for i in range(c) for b in range(k // sub)] +
            [v2[j] + st1_tail[b] for j in range(c) for b in range(k // sub)])
    cv = _top_sorted([cand], k)[0]
    sel = [x >= cv[k - 1] for x in cand]
    e1s = [jnp.exp(x - v1[0]) for x in st1]
    e2s = [jnp.exp(x - v2[0]) for x in st2]
    prod = ([jnp.exp(v1[i] - v1[0]) * e2s[b] for i in range(c) for b in range(k // sub)] +
            [jnp.exp(v2[j] - v2[0]) * e1s[b] for j in range(c) for b in range(k // sub)])
    z = _rep_sum(functools.reduce(lambda a, b: a + b, [jnp.where(s, p, 0.0) for s, p in zip(sel, prod)]))
    ones = [jnp.where(s, 1.0, 0.0) for s in sel]
    per = k // sub
    tail = [functools.reduce(lambda a, b: a + b, [ones[(c + j) * per + b] for j in range(c)]) for b in range(per)]
    counts = []
    for i in range(k):
        if i < c:
            counts.append(_rep_sum(functools.reduce(lambda a, b: a + b, ones[i * per:(i + 1) * per])))
        else:
            counts.append(_rep_sum(jnp.where(srow == i % sub, tail[i // sub], 0.0)))
    inv_z = 1.0 / z
    ja, cf, r2, e2 = [], [], [], []
    for q in range(nblk):
        jq = jnp.zeros(b1[q].shape, F32)
        rq = jnp.zeros(b2[q].shape, F32)
        for i in range(k):
            jq = jnp.where(b1[q] == v1[i], counts[i], jq)
            rq = jnp.where(b2[q] < v2[i], float(i + 1), rq)
        ja.append(jq)
        r2.append(rq)
        cf.append(jnp.exp(b1[q] - v1[0]) * inv_z)
        e2.append(jnp.exp(b2[q] - v2[0]))

    def picked(blocks, thr):
        return _rep_sum(functools.reduce(lambda a, b: a + b, [jnp.where(x >= thr, 1.0, 0.0) for x in blocks]))

    def repeats(vals):
        return functools.reduce(lambda a, b: a | b, [vals[i] == vals[i + 1] for i in range(k - 1)])

    tie = ((picked(b1, v1[k - 1]) != k) | (picked(b2, v2[k - 1]) != k) | (picked(cand, cv[k - 1]) != k) |
           repeats(v1) | repeats(v2) | repeats(cv))
    cat = lambda blocks: jnp.concatenate(blocks, axis=0)
    return (cat(r2).astype(BF16), cat(e2).astype(BF16), cat(ja), cat(cf)), tie.astype(jnp.int32)


def _route_kernel(ht_ref, wqt_ref, keys_ref, r2_ref, e2_ref, ja_ref, cf_ref, qt_scr):
    nk = PEER_N_KEYS
    tm = ht_ref.shape[1]
    qt_scr[...] = jnp.dot(wqt_ref[...], ht_ref[...], preferred_element_type=F32).astype(BF16)

    def head(h, carry):
        q0 = pl.multiple_of(h * 2 * nk, 2 * nk)
        s1 = jnp.dot(keys_ref[0], qt_scr[pl.ds(q0, nk), :], preferred_element_type=F32)
        s2 = jnp.dot(keys_ref[1], qt_scr[pl.ds(q0 + nk, nk), :], preferred_element_type=F32)
        def store(cols, res):
            r2_ref[h, :, cols], e2_ref[h, :, cols], ja_ref[h, :, cols], cf_ref[h, :, cols] = res

        for g in range(tm // LANES):
            cols = slice(g * LANES, (g + 1) * LANES)
            res, tie = _route_sorted(s1[:, cols], s2[:, cols])
            store(cols, res)

            @pl.when(jnp.max(tie) > 0)
            def _(cols=cols):
                store(cols, _route_exact(s1[:, cols], s2[:, cols]))
        return carry

    lax.fori_loop(0, PEER_HEADS, head, 0)


def _peer_route(ht, wqt, keys_bf, tm=512):
    d, t = ht.shape
    shp = lambda dt: jax.ShapeDtypeStruct((PEER_HEADS, PEER_N_KEYS, t), dt)
    spec = pl.BlockSpec((PEER_HEADS, PEER_N_KEYS, tm), lambda i: (0, 0, i))
    return pl.pallas_call(
        _route_kernel,
        grid=(t // tm,),
        in_specs=[pl.BlockSpec((d, tm), lambda i: (0, i)),
                  pl.BlockSpec(wqt.shape, lambda i: (0, 0)),
                  pl.BlockSpec(keys_bf.shape, lambda i: (0, 0, 0))],
        out_specs=[spec] * 4,
        out_shape=[shp(BF16), shp(BF16), shp(F32), shp(F32)],
        scratch_shapes=[pltpu.VMEM((wqt.shape[0], tm), BF16)],
        compiler_params=_cparams("parallel"),
        name="peer_route",
    )(ht, wqt, keys_bf)


EXP_CHUNK = 512
GATE_LANES = 256

def _gated_hidden(gel, a0, r2_ref, e2_ref, ja_ref, cf_ref, w_ref):
    nk = PEER_N_KEYS
    na = gel.shape[0] // nk
    zero = jnp.zeros((), BF16)
    tm = gel.shape[1]

    def row(ref, h, aa):
        return ref[h, pl.ds(a0 + aa, 1), :].astype(BF16)

    ja_rows = [[row(ja_ref, h, aa) for aa in range(na)] for h in range(PEER_HEADS)]
    cf_rows = [[row(cf_ref, h, aa) for aa in range(na)] for h in range(PEER_HEADS)]
    for lg in range(tm // GATE_LANES):
        lanes = slice(lg * GATE_LANES, (lg + 1) * GATE_LANES)
        acc = [None] * na
        for h in range(PEER_HEADS):
            r2c, e2c = r2_ref[h, :, lanes], e2_ref[h, :, lanes]
            for aa in range(na):
                term = jnp.where(r2c < ja_rows[h][aa][:, lanes], e2c * cf_rows[h][aa][:, lanes], zero)
                acc[aa] = term if acc[aa] is None else acc[aa] + term
        for aa in range(na):
            rows = slice(aa * nk, (aa + 1) * nk)
            w_ref[rows, lanes] = acc[aa] * gel[rows, lanes]


def _experts_kernel(ht_ref, u_ref, vt_ref, r2_ref, e2_ref, ja_ref, cf_ref, yt_ref, w_scr):
    j = pl.program_id(1)
    te = u_ref.shape[0]

    @pl.when(j == 0)
    def _():
        yt_ref[...] = jnp.zeros_like(yt_ref)

    ht = ht_ref[...]
    nchunk = te // EXP_CHUNK

    def hidden(c):
        return jnp.dot(u_ref[c * EXP_CHUNK:(c + 1) * EXP_CHUNK, :], ht, preferred_element_type=F32)

    act = hidden(0)
    for c in range(nchunk):
        nxt = hidden(c + 1) if c + 1 < nchunk else None
        gel = (0.5 * act * (1.0 + lax.erf(act * (2.0 ** -0.5)))).astype(BF16)
        a0 = j * (te // PEER_N_KEYS) + c * (EXP_CHUNK // PEER_N_KEYS)
        _gated_hidden(gel, a0, r2_ref, e2_ref, ja_ref, cf_ref, w_scr.at[c])
        yt_ref[...] += jnp.dot(vt_ref[:, c * EXP_CHUNK:(c + 1) * EXP_CHUNK], w_scr[c],
                               preferred_element_type=F32)
        act = nxt


def _peer_experts(ht, u_bf, vt_bf, r2, e2, ja, cf, tm=512, te=1024):
    d, t = ht.shape
    ne = u_bf.shape[0]
    side = pl.BlockSpec((PEER_HEADS, PEER_N_KEYS, tm), lambda i, j: (0, 0, i))
    return pl.pallas_call(
        _experts_kernel,
        grid=(t // tm, ne // te),
        in_specs=[pl.BlockSpec((d, tm), lambda i, j: (0, i)),
                  pl.BlockSpec((te, d), lambda i, j: (j, 0)),
                  pl.BlockSpec((d, te), lambda i, j: (0, j)),
                  side, side, side, side],
        out_specs=pl.BlockSpec((d, tm), lambda i, j: (0, i)),
        out_shape=jax.ShapeDtypeStruct((d, t), F32),
        scratch_shapes=[pltpu.VMEM((te // EXP_CHUNK, EXP_CHUNK, tm), BF16)],
        compiler_params=_cparams("parallel", "arbitrary"),
        name="peer_experts",
    )(ht, u_bf, vt_bf, r2, e2, ja, cf)


def _resid_ln_kernel(x_ref, yt_ref, gf_ref, lg_ref, lb_ref, o_ref, *, alpha):
    o_ref[...] = _layer_norm(alpha * x_ref[...] + (1.0 + gf_ref[0]) * yt_ref[...].T, lg_ref[...], lb_ref[...])


def _resid_ln(x2d, yt, mod3, ln_g, ln_b, seq, alpha, tm=512):
    t, d = x2d.shape
    per_b = seq // tm
    row = pl.BlockSpec((tm, d), lambda i: (i, 0))
    vec = pl.BlockSpec((1, d), lambda i: (0, 0))
    return pl.pallas_call(
        functools.partial(_resid_ln_kernel, alpha=alpha),
        grid=(t // tm,),
        in_specs=[row, pl.BlockSpec((d, tm), lambda i: (0, i)),
                  pl.BlockSpec((1, 1, d), lambda i: (i // per_b, 0, 5)), vec, vec],
        out_specs=row,
        out_shape=jax.ShapeDtypeStruct((t, d), F32),
        compiler_params=_cparams("parallel"),
        name="resid_ln",
    )(x2d, yt, mod3, ln_g.reshape(1, d), ln_b.reshape(1, d))


def kernel(x, c, w_mod, b_mod, w_in, a_q_gain, a_k_gain, c_sink, w_pa, w_pb, w_pc, w_o,
           ln1_g, ln1_b, peer_wq, peer_keys, peer_u, peer_v, ln2_g, ln2_b):
    nb, seq, d = x.shape
    depth = w_mod.shape[0]
    alpha = float((2 * depth) ** 0.25)
    ctab, stab = _rope_tables(seq)
    mod = _modulation(c, w_mod, b_mod)
    x2d = x.reshape(nb * seq, d)
    for l in range(depth):
        mod3 = mod[l].reshape(nb, 1, 6 * d)
        qkv_a, qkv_b, qkv_c, gl = _in_projection(x2d, mod3, w_in, l, seq)
        oa = _mixer_a(qkv_a, ctab, stab, a_q_gain[l], a_k_gain[l], nb, seq)
        ob = _mixer_b(qkv_b, nb, seq)
        oc = _mixer_c(qkv_c, c_sink[l], nb, seq)
        x1, ht = _merge_project(oa, ob, oc, gl, x2d, mod3, ln1_g[l], ln1_b[l],
                                w_pa[l].astype(BF16), w_pb[l].astype(BF16), w_pc[l].astype(BF16),
                                w_o[l].astype(BF16), seq, alpha)
        r2, e2, ja, cf = _peer_route(ht, peer_wq[l].T.astype(BF16), peer_keys[l].astype(BF16))
        yt = _peer_experts(ht, peer_u[l].astype(BF16), peer_v[l].T.astype(BF16), r2, e2, ja, cf)
        x2d = _resid_ln(x1, yt, mod3, ln2_g[l], ln2_b[l], seq, alpha)
    return x2d.reshape(nb, seq, d)
```

```python
import functools

import numpy as np
import jax
import jax.numpy as jnp
from jax import lax
from jax.experimental import pallas as pl
from jax.experimental.pallas import tpu as pltpu

F32 = jnp.float32
BF16 = jnp.bfloat16

HEAD_DIM = 128
A_Q_HEADS, A_KV_HEADS = 8, 2
B_GROUPS = ((128, 1), (512, 4), (2048, 16))
B_HEADS_PER_GROUP = 2
B_HEADS = B_HEADS_PER_GROUP * len(B_GROUPS)
C_Q_HEADS, C_KV_HEADS = 8, 2
C_HALF_WINDOW = 128
N_BRANCHES = 3
GRID_W = 64
ROPE_THETA = 10000.0
PEER_HEADS = 8
PEER_N_KEYS = 128
PEER_TOPK = 16
LN_EPS = 1e-5
QK_EPS = 1e-6
NEG_INF = -1e30

LANES = 128
SUBLANES = 8
BF16_ROWS = 16
VMEM_LIMIT = 56 * 1024 * 1024
MERGE_VMEM_LIMIT = 58 * 1024 * 1024


def _cparams(*sem, vmem=VMEM_LIMIT):
    return pltpu.CompilerParams(dimension_semantics=sem, vmem_limit_bytes=vmem)


def _mod_kernel(cb_ref, w_ref, b_ref, o_ref):
    nb, tn = cb_ref.shape[0], w_ref.shape[2]
    rows = []
    for b in range(nb):
        cb = cb_ref[b]
        pieces = [jnp.sum(w_ref[0, :, n * LANES:(n + 1) * LANES] * cb, axis=0, keepdims=True)
                  for n in range(tn // LANES)]
        rows.append(jnp.concatenate(pieces, axis=1))
    o_ref[0] = jnp.concatenate(rows, axis=0) + b_ref[0]


def _modulation(c, w_mod, b_mod, tn=1024):
    nl, d, n6 = w_mod.shape
    nb = c.shape[0]
    cb = jnp.broadcast_to(c[:, :, None], (nb, d, LANES))
    return pl.pallas_call(
        _mod_kernel,
        grid=(nl, n6 // tn),
        in_specs=[pl.BlockSpec((nb, d, LANES), lambda l, j: (0, 0, 0)),
                  pl.BlockSpec((1, d, tn), lambda l, j: (l, 0, j)),
                  pl.BlockSpec((1, 1, tn), lambda l, j: (l, 0, j))],
        out_specs=pl.BlockSpec((1, nb, tn), lambda l, j: (l, 0, j)),
        out_shape=jax.ShapeDtypeStruct((nl, nb, n6), F32),
        compiler_params=_cparams("parallel", "parallel"),
        name="adaln_modulation",
    )(cb, w_mod, b_mod.reshape(nl, 1, n6))


IN_TN = 768
IN_TILES = (2, 3, 2, 8)


def _inproj_kernel(x_ref, sc_ref, sh_ref, w_ref, oa_ref, ob_ref, oc_ref, og_ref, h_scr):
    j = pl.program_id(1)

    @pl.when(j == 0)
    def _():
        h_scr[...] = (x_ref[...] * (1.0 + sc_ref[0]) + sh_ref[0]).astype(BF16)

    lo = 0
    for ref, n in zip((oa_ref, ob_ref, oc_ref, og_ref), IN_TILES):
        @pl.when((j >= lo) & (j < lo + n))
        def _(ref=ref):
            ref[...] = jnp.dot(h_scr[...], w_ref[...].astype(BF16), preferred_element_type=F32).astype(BF16)
        lo += n


def _in_projection(x2d, mod3, w_all, layer, seq, tm=1024):
    t, d = x2d.shape
    tm = min(tm, seq)
    per_b = seq // tm
    starts = np.cumsum((0,) + IN_TILES[:-1]).tolist()

    def out_spec(lo, n):
        return pl.BlockSpec((tm, IN_TN), lambda i, j: (i, jnp.clip(j - lo, 0, n - 1)))

    return pl.pallas_call(
        _inproj_kernel,
        grid=(t // tm, sum(IN_TILES)),
        in_specs=[pl.BlockSpec((tm, d), lambda i, j: (i, 0)),
                  pl.BlockSpec((1, 1, d), lambda i, j: (i // per_b, 0, 1)),
                  pl.BlockSpec((1, 1, d), lambda i, j: (i // per_b, 0, 0)),
                  pl.BlockSpec((None, d, IN_TN), lambda i, j: (layer, 0, j))],
        out_specs=[out_spec(lo, n) for lo, n in zip(starts, IN_TILES)],
        out_shape=[jax.ShapeDtypeStruct((t, n * IN_TN), BF16) for n in IN_TILES],
        scratch_shapes=[pltpu.VMEM((tm, d), BF16)],
        compiler_params=_cparams("parallel", "arbitrary"),
        name="in_projection",
    )(x2d, mod3, mod3, w_all)


def _rope_tables(seq):
    rows = seq // GRID_W
    row = jnp.repeat(jnp.arange(rows, dtype=F32), GRID_W)
    col = jnp.tile(jnp.arange(GRID_W, dtype=F32), rows)
    quarter = HEAD_DIM // 4
    inv = ROPE_THETA ** (-jnp.arange(quarter, dtype=F32) / quarter)
    ar, ac = row[:, None] * inv, col[:, None] * inv
    ctab = jnp.concatenate([jnp.cos(ar), jnp.cos(ar), jnp.cos(ac), jnp.cos(ac)], axis=1)
    stab = jnp.concatenate([-jnp.sin(ar), jnp.sin(ar), -jnp.sin(ac), jnp.sin(ac)], axis=1)
    return ctab, stab


def _norm_rope(xf, gain, ctab, stab):
    hd = xf.shape[1]
    r = lax.broadcasted_iota(jnp.int32, (hd, hd), 0)
    c = lax.broadcasted_iota(jnp.int32, (hd, hd), 1)
    mean_mat = jnp.full((hd, hd), 1.0 / hd, BF16)
    swap_mat = jnp.where(r == (c ^ (hd // 4)), 1.0, 0.0).astype(BF16)
    ms = jnp.dot((xf * xf).astype(BF16), mean_mat, preferred_element_type=F32)
    xn = xf * lax.rsqrt(ms + QK_EPS) * gain
    partner = jnp.dot(xn.astype(BF16), swap_mat, preferred_element_type=F32)
    return xn * ctab + partner * stab


def _attn_a_kernel(q_ref, k_ref, v_ref, cq_ref, sq_ref, ck_ref, sk_ref, qg_ref, kg_ref,
                   o_ref, kp_scr, vp_scr, *, kc):
    tq, seq = q_ref.shape[0], k_ref.shape[0]
    grp = q_ref.shape[1] // HEAD_DIM

    @pl.when(pl.program_id(2) == 0)
    def _():
        kp_scr[...] = _norm_rope(k_ref[...].astype(F32), kg_ref[...], ck_ref[...], sk_ref[...]).astype(BF16)
        vp_scr[:, :HEAD_DIM] = v_ref[...]
        vp_scr[:, HEAD_DIM:] = jnp.ones((seq, HEAD_DIM), BF16)

    cq, sq, qg = cq_ref[...], sq_ref[...], qg_ref[...]
    scale = HEAD_DIM ** -0.5
    q4 = jnp.concatenate(
        [(_norm_rope(q_ref[:, h * HEAD_DIM:(h + 1) * HEAD_DIM].astype(F32), qg, cq, sq) * scale).astype(BF16)
         for h in range(grp)], axis=0)

    def scores(c):
        return lax.dot_general(q4, kp_scr[c * kc:(c + 1) * kc, :], (((1,), (1,)), ((), ())),
                               preferred_element_type=F32)

    nchunk = seq // kc
    s = scores(0)
    m = acc = None
    for c in range(nchunk):
        nxt = scores(c + 1) if c + 1 < nchunk else None
        smax = jnp.max(s, axis=-1, keepdims=True)
        mn = smax if m is None else jnp.maximum(m, smax)
        p = jnp.exp(s - mn).astype(BF16)
        pv = jnp.dot(p, vp_scr[c * kc:(c + 1) * kc, :], preferred_element_type=F32)
        acc = pv if m is None else jnp.exp(m - mn) * acc + pv
        m, s = mn, nxt
    o = acc[:, :HEAD_DIM] / acc[:, HEAD_DIM:HEAD_DIM + 1]
    for h in range(grp):
        o_ref[:, h * HEAD_DIM:(h + 1) * HEAD_DIM] = o[h * tq:(h + 1) * tq].astype(BF16)


def _mixer_a(qkv, ctab, stab, q_gain, k_gain, nb, seq, tq=256, kc=512):
    t = qkv.shape[0]
    nq = seq // tq
    gw = (A_Q_HEADS // A_KV_HEADS) * HEAD_DIM
    k_blk = A_Q_HEADS
    v_blk = A_Q_HEADS + A_KV_HEADS
    return pl.pallas_call(
        functools.partial(_attn_a_kernel, kc=kc),
        grid=(nb, A_KV_HEADS, nq),
        in_specs=[pl.BlockSpec((tq, gw), lambda b, g, i: (b * nq + i, g)),
                  pl.BlockSpec((seq, HEAD_DIM), lambda b, g, i: (b, k_blk + g)),
                  pl.BlockSpec((seq, HEAD_DIM), lambda b, g, i: (b, v_blk + g)),
                  pl.BlockSpec((tq, HEAD_DIM), lambda b, g, i: (i, 0)),
                  pl.BlockSpec((tq, HEAD_DIM), lambda b, g, i: (i, 0)),
                  pl.BlockSpec((seq, HEAD_DIM), lambda b, g, i: (0, 0)),
                  pl.BlockSpec((seq, HEAD_DIM), lambda b, g, i: (0, 0)),
                  pl.BlockSpec((1, HEAD_DIM), lambda b, g, i: (0, 0)),
                  pl.BlockSpec((1, HEAD_DIM), lambda b, g, i: (0, 0))],
        out_specs=pl.BlockSpec((tq, gw), lambda b, g, i: (b * nq + i, g)),
        out_shape=jax.ShapeDtypeStruct((t, A_Q_HEADS * HEAD_DIM), BF16),
        scratch_shapes=[pltpu.VMEM((seq, HEAD_DIM), BF16), pltpu.VMEM((seq, 2 * HEAD_DIM), BF16)],
        compiler_params=_cparams("parallel", "parallel", "arbitrary"),
        name="mixer_a_axial_gqa",
    )(qkv, qkv, qkv, ctab, stab, ctab, stab, q_gain.reshape(1, HEAD_DIM), k_gain.reshape(1, HEAD_DIM))


def _window(q0, tq, half, seq):
    span = min(tq + 2 * half, seq)
    start = jnp.clip(q0 - half, 0, seq - span)
    return pl.multiple_of(start, 64), span


def _band_kernel(q_ref, k_ref, v_ref, o_ref, lse_ref, *, half, dist_scale, slopes):
    tq, length = q_ref.shape[0], k_ref.shape[0]
    q0 = pl.program_id(2) * tq
    start, span = _window(q0, tq, half, length)
    qpos = q0 + lax.broadcasted_iota(jnp.int32, (tq, span), 0)
    kpos = start + lax.broadcasted_iota(jnp.int32, (tq, span), 1)
    ad = jnp.abs(qpos - kpos)
    valid = ad <= half
    dist = (ad * dist_scale).astype(F32)
    scale = HEAD_DIM ** -0.5
    for hh, slope in enumerate(slopes):
        cols = slice(hh * HEAD_DIM, (hh + 1) * HEAD_DIM)
        s = lax.dot_general(q_ref[:, cols], k_ref[pl.ds(start, span), cols],
                            (((1,), (1,)), ((), ())), preferred_element_type=F32) * scale
        s = jnp.where(valid, s - slope * dist, NEG_INF)
        m = jnp.max(s, axis=-1, keepdims=True)
        p = jnp.exp(s - m)
        l = jnp.sum(p, axis=-1, keepdims=True)
        o = jnp.dot(p.astype(BF16), v_ref[pl.ds(start, span), cols], preferred_element_type=F32)
        o_ref[:, cols] = (o / l).astype(BF16)
        lse_ref[:, cols] = jnp.broadcast_to(m + jnp.log(l), (tq, HEAD_DIM))


def _mixer_b(qkv, nb, seq, tq=256):
    t, width = qkv.shape
    gw = B_HEADS_PER_GROUP * HEAD_DIM
    per_row = width // gw
    outs = []
    for g, (win, r) in enumerate(B_GROUPS):
        length = seq // r
        tqg = min(tq, length)
        nq = length // tqg
        view = qkv.reshape(t // r, r * width)
        heads = range(g * B_HEADS_PER_GROUP, (g + 1) * B_HEADS_PER_GROUP)
        slopes = tuple(float(2.0 ** (-8.0 * (h + 1) / B_HEADS)) for h in heads)
        kv = lambda part, g=g: pl.BlockSpec((length, gw), lambda b, res, i: (b, res * per_row + part * len(B_GROUPS) + g))
        o, lse = pl.pallas_call(
            functools.partial(_band_kernel, half=(win // 2) // r, dist_scale=r, slopes=slopes),
            grid=(nb, r, nq),
            in_specs=[pl.BlockSpec((tqg, gw), lambda b, res, i, g=g, nq=nq: (b * nq + i, res * per_row + g)),
                      kv(1), kv(2)],
            out_specs=[pl.BlockSpec((tqg, gw), lambda b, res, i, nq=nq: (b * nq + i, res))] * 2,
            out_shape=[jax.ShapeDtypeStruct((t // r, r * gw), BF16), jax.ShapeDtypeStruct((t // r, r * gw), F32)],
            compiler_params=_cparams("parallel", "parallel", "arbitrary"),
            name=f"mixer_b_dilation{r}",
        )(view, view, view)
        outs.append((o.reshape(t, gw), lse.reshape(t, gw)))
    return outs


def _attn_c_kernel(hp_ref, q_ref, k_ref, v_ref, o_ref):
    tq, seq = q_ref.shape[0], k_ref.shape[0]
    grp = q_ref.shape[1] // HEAD_DIM
    g = pl.program_id(1)
    q0 = pl.program_id(2) * tq
    start, span = _window(q0, tq, C_HALF_WINDOW, seq)
    qpos = q0 + lax.broadcasted_iota(jnp.int32, (tq, span), 0)
    kpos = start + lax.broadcasted_iota(jnp.int32, (tq, span), 1)
    ad = jnp.abs(qpos - kpos)
    valid = ad <= C_HALF_WINDOW
    adf = ad.astype(F32)
    kk = k_ref[pl.ds(start, span), :]
    vv = v_ref[pl.ds(start, span), :]
    scale = HEAD_DIM ** -0.5
    for hh in range(grp):
        cols = slice(hh * HEAD_DIM, (hh + 1) * HEAD_DIM)
        slope, sink = hp_ref[g, hh], hp_ref[g, grp + hh]
        s = lax.dot_general(q_ref[:, cols], kk, (((1,), (1,)), ((), ())),
                            preferred_element_type=F32) * scale
        s = jnp.where(valid, s - slope * adf, NEG_INF)
        m = jnp.maximum(jnp.max(s, axis=-1, keepdims=True), sink)
        p = jnp.exp(s - m)
        l = jnp.sum(p, axis=-1, keepdims=True) + jnp.exp(sink - m)
        o = jnp.dot(p.astype(BF16), vv, preferred_element_type=F32)
        o_ref[:, cols] = (o / l).astype(BF16)


def _mixer_c(qkv, sink, nb, seq, tq=256):
    t = qkv.shape[0]
    nq = seq // tq
    grp = C_Q_HEADS // C_KV_HEADS
    gw = grp * HEAD_DIM
    slopes = (2.0 ** (-8.0 * jnp.arange(1, C_Q_HEADS + 1, dtype=F32) / C_Q_HEADS)).reshape(C_KV_HEADS, grp)
    head_params = jnp.concatenate([slopes, sink.astype(F32)], axis=1)
    return pl.pallas_call(
        _attn_c_kernel,
        grid=(nb, C_KV_HEADS, nq),
        in_specs=[pl.BlockSpec(memory_space=pltpu.SMEM),
                  pl.BlockSpec((tq, gw), lambda b, g, i: (b * nq + i, g)),
                  pl.BlockSpec((seq, HEAD_DIM), lambda b, g, i: (b, C_Q_HEADS + g)),
                  pl.BlockSpec((seq, HEAD_DIM), lambda b, g, i: (b, C_Q_HEADS + C_KV_HEADS + g))],
        out_specs=pl.BlockSpec((tq, gw), lambda b, g, i: (b * nq + i, g)),
        out_shape=jax.ShapeDtypeStruct((t, C_Q_HEADS * HEAD_DIM), BF16),
        compiler_params=_cparams("parallel", "parallel", "arbitrary"),
        name="mixer_c_window_sink",
    )(head_params, qkv, qkv, qkv)


def _layer_norm(z, g, b):
    mu = jnp.mean(z, axis=-1, keepdims=True)
    zc = z - mu
    var = jnp.mean(zc * zc, axis=-1, keepdims=True)
    return zc * lax.rsqrt(var + LN_EPS) * g + b


def _branch_kernel(oa_ref, b0_ref, b1_ref, b2_ref, l0_ref, l1_ref, l2_ref, oc_ref, gl_ref,
                   wa_ref, wb_ref, wc_ref, m_ref):
    d = m_ref.shape[1]
    lses = [r[...] for r in (l0_ref, l1_ref, l2_ref)]
    top = functools.reduce(jnp.maximum, lses)
    ex = [jnp.exp(x - top) for x in lses]
    inv = 1.0 / functools.reduce(lambda a, b: a + b, ex)
    ob = jnp.concatenate([(o_ref[...].astype(F32) * (e * inv)).astype(BF16)
                          for o_ref, e in zip((b0_ref, b1_ref, b2_ref), ex)], axis=1)
    merged = None
    for i, (o, w_ref) in enumerate(((oa_ref[...], wa_ref), (ob, wb_ref), (oc_ref[...], wc_ref))):
        y = jnp.dot(o, w_ref[...], preferred_element_type=F32)
        term = jax.nn.sigmoid(gl_ref[:, i * d:(i + 1) * d].astype(F32)) * y
        merged = term if merged is None else merged + term
    m_ref[...] = merged.astype(BF16)


def _outproj_kernel(m_ref, x_ref, ga_ref, scf_ref, shf_ref, lg_ref, lb_ref, wo_ref, x1_ref, ht_ref, *, alpha):
    y = jnp.dot(m_ref[...], wo_ref[...], preferred_element_type=F32)
    x1 = _layer_norm(alpha * x_ref[...] + (1.0 + ga_ref[0]) * y, lg_ref[...], lb_ref[...])
    x1_ref[...] = x1
    h2 = x1 * (1.0 + scf_ref[0]) + shf_ref[0]
    ht_ref[...] = h2.T.astype(BF16)


def _merge_project(oa, ob, oc, gl, x2d, mod3, ln_g, ln_b, wa, wb, wc, wo, seq, alpha, tm=512):
    t, d = x2d.shape
    per_b = seq // tm
    b_outs, b_lses = [o for o, _ in ob], [lse for _, lse in ob]
    row = lambda w: pl.BlockSpec((tm, w), lambda i: (i, 0))
    whole = lambda a: pl.BlockSpec(a.shape, lambda i: (0, 0), pipeline_mode=pl.Buffered(1))
    modspec = lambda k: pl.BlockSpec((1, 1, d), lambda i: (i // per_b, 0, k))
    vec = pl.BlockSpec((1, d), lambda i: (0, 0))
    merged = pl.pallas_call(
        _branch_kernel,
        grid=(t // tm,),
        in_specs=[row(oa.shape[1])] + [row(a.shape[1]) for a in b_outs + b_lses] +
                 [row(oc.shape[1]), row(gl.shape[1]), whole(wa), whole(wb), whole(wc)],
        out_specs=row(d),
        out_shape=jax.ShapeDtypeStruct((t, d), BF16),
        compiler_params=_cparams("parallel"),
        name="branch_merge",
    )(oa, *b_outs, *b_lses, oc, gl, wa, wb, wc)
    return pl.pallas_call(
        functools.partial(_outproj_kernel, alpha=alpha),
        grid=(t // tm,),
        in_specs=[row(d), row(d), modspec(2), modspec(4), modspec(3), vec, vec, whole(wo)],
        out_specs=[pl.BlockSpec((tm, d), lambda i: (i, 0)), pl.BlockSpec((d, tm), lambda i: (0, i))],
        out_shape=[jax.ShapeDtypeStruct((t, d), F32), jax.ShapeDtypeStruct((d, t), BF16)],
        compiler_params=_cparams("parallel"),
        name="outproj_ln",
    )(merged, x2d, mod3, mod3, mod3, ln_g.reshape(1, d), ln_b.reshape(1, d), wo)


CAND_ROWS = 4


def _top_exact(s, n, order):
    nrow = lax.broadcasted_iota(jnp.int32, (n, s.shape[1]), 0)

    def step(i, carry):
        s, rank, vals = carry
        m = jnp.max(s, axis=0, keepdims=True)
        first = jnp.min(jnp.where(s == m, order, jnp.iinfo(jnp.int32).max), axis=0, keepdims=True)
        hit = order == first
        return jnp.where(hit, -jnp.inf, s), jnp.where(hit, i, rank), jnp.where(nrow == i, m, vals)

    _, rank, vals = lax.fori_loop(0, n, step, (s, jnp.full(s.shape, n, jnp.int32), jnp.zeros(nrow.shape, F32)))
    return vals, rank


def _candidates(v1, v2):
    k, c = PEER_TOPK, CAND_ROWS
    ki = lax.broadcasted_iota(jnp.int32, v1.shape, 0)
    v1_tail = jnp.where(ki >= c, v1, -jnp.inf)
    cand = jnp.concatenate([v1[i:i + 1] + v2 for i in range(c)] + [v2[j:j + 1] + v1_tail for j in range(c)], axis=0)
    flat = jnp.concatenate([ki + i * k for i in range(c)] +
                           [jnp.where(ki >= c, ki * k + j, -1) for j in range(c)], axis=0)
    return cand, flat


def _route_outputs(s1, s2, v1, v2, rank2, sel, is_ith):
    k, c = PEER_TOPK, CAND_ROWS
    e1 = jnp.exp(v1 - v1[0:1])
    e2 = jnp.exp(v2 - v2[0:1])
    prod = jnp.concatenate([e1[i:i + 1] * e2 for i in range(c)] + [e2[j:j + 1] * e1 for j in range(c)], axis=0)
    z = jnp.sum(jnp.where(sel, prod, 0.0), axis=0, keepdims=True)
    self32 = sel.astype(F32)
    tail = functools.reduce(lambda a, b: a + b, [self32[(c + j) * k:(c + j + 1) * k] for j in range(c)])
    ja = jnp.zeros(s1.shape, F32)
    for i in range(k):
        cnt = jnp.sum(self32[i * k:(i + 1) * k], axis=0, keepdims=True) if i < c else tail[i:i + 1]
        ja = jnp.where(is_ith(i), cnt, ja)
    return (rank2.astype(F32).astype(BF16), jnp.exp(s2 - v2[0:1]).astype(BF16), ja,
            jnp.exp(s1 - v1[0:1]) / z)


def _route_exact(s1, s2):
    k = PEER_TOPK
    rows = lax.broadcasted_iota(jnp.int32, s1.shape, 0)
    v1, rank1 = _top_exact(s1, k, rows)
    v2, rank2 = _top_exact(s2, k, rows)
    cand, flat = _candidates(v1, v2)
    _, crank = _top_exact(cand, k, flat)
    return _route_outputs(s1, s2, v1, v2, rank2, crank < k, lambda i: rank1 == i)


def _sort_network(n):
    def merge(lo, hi, r):
        step = r * 2
        if step < hi - lo:
            yield from merge(lo, hi, step)
            yield from merge(lo + r, hi, step)
            yield from [(i, i + r) for i in range(lo + r, hi - r, step)]
        else:
            yield (lo, lo + r)

    def sort(lo, hi):
        if hi - lo >= 1:
            mid = lo + (hi - lo) // 2
            yield from sort(lo, mid)
            yield from sort(mid + 1, hi)
            yield from merge(lo, hi, 1)

    return tuple(sort(0, n - 1))


def _rep_max(x):
    return jnp.broadcast_to(jnp.max(x, axis=0, keepdims=True), x.shape)


def _rep_sum(x):
    return jnp.broadcast_to(jnp.sum(x, axis=0, keepdims=True), x.shape)


def _top_sorted(groups, n):
    work = [list(g) for g in groups]
    for i, j in _sort_network(len(work[0])):
        for v in work:
            v[i], v[j] = jnp.maximum(v[i], v[j]), jnp.minimum(v[i], v[j])
    vals = [[] for _ in work]
    for it in range(n):
        for v, out in zip(work, vals):
            m = _rep_max(v[0])
            out.append(m)
            hit = v[0] == m
            for k in range(n - 1 - it):
                v[k] = jnp.where(hit, v[k + 1], v[k])
    return vals


def _route_sorted(s1, s2):
    k, c, sub = PEER_TOPK, CAND_ROWS, SUBLANES
    nblk = s1.shape[0] // sub
    b1 = [s1[sub * q:sub * (q + 1)] for q in range(nblk)]
    b2 = [s2[sub * q:sub * (q + 1)] for q in range(nblk)]
    v1, v2 = _top_sorted([b1, b2], k)
    srow = lax.broadcasted_iota(jnp.int32, b1[0].shape, 0)

    def stack(vals):
        return [functools.reduce(lambda acc, t: jnp.where(srow == t, vals[blk * sub + t], acc),
                                 range(1, sub), vals[blk * sub]) for blk in range(k // sub)]

    st1, st2 = stack(v1), stack(v2)
    st1_tail = [jnp.where(srow >= c, st1[0], -jnp.inf)] + st1[1:]
    cand = ([v1[i] + st2[b] for i in range(c) for b in range(k // sub)] +
            [v2[j] + st1_tail[b] for j in range(c) for b in range(k // sub)])
    cv = _top_sorted([cand], k)[0]
    sel = [x >= cv[k - 1] for x in cand]
    e1s = [jnp.exp(x - v1[0]) for x in st1]
    e2s = [jnp.exp(x - v2[0]) for x in st2]
    prod = ([jnp.exp(v1[i] - v1[0]) * e2s[b] for i in range(c) for b in range(k // sub)] +
            [jnp.exp(v2[j] - v2[0]) * e1s[b] for j in range(c) for b in range(k // sub)])
    z = _rep_sum(functools.reduce(lambda a, b: a + b, [jnp.where(s, p, 0.0) for s, p in zip(sel, prod)]))
    ones = [jnp.where(s, 1.0, 0.0) for s in sel]
    per = k // sub
    tail = [functools.reduce(lambda a, b: a + b, [ones[(c + j) * per + b] for j in range(c)]) for b in range(per)]
    counts = []
    for i in range(k):
        if i < c:
            counts.append(_rep_sum(functools.reduce(lambda a, b: a + b, ones[i * per:(i + 1) * per])))
        else:
            counts.append(_rep_sum(jnp.where(srow == i % sub, tail[i // sub], 0.0)))
    inv_z = 1.0 / z
    ja, cf, r2, e2 = [], [], [], []
    for q in range(nblk):
        jq = jnp.zeros(b1[q].shape, F32)
        rq = jnp.zeros(b2[q].shape, F32)
        for i in range(k):
            jq = jnp.where(b1[q] == v1[i], counts[i], jq)
            rq = jnp.where(b2[q] < v2[i], float(i + 1), rq)
        ja.append(jq)
        r2.append(rq)
        cf.append(jnp.exp(b1[q] - v1[0]) * inv_z)
        e2.append(jnp.exp(b2[q] - v2[0]))

    def picked(blocks, thr):
        return _rep_sum(functools.reduce(lambda a, b: a + b, [jnp.where(x >= thr, 1.0, 0.0) for x in blocks]))

    def repeats(vals):
        return functools.reduce(lambda a, b: a | b, [vals[i] == vals[i + 1] for i in range(k - 1)])

    tie = ((picked(b1, v1[k - 1]) != k) | (picked(b2, v2[k - 1]) != k) | (picked(cand, cv[k - 1]) != k) |
           repeats(v1) | repeats(v2) | repeats(cv))
    cat = lambda blocks: jnp.concatenate(blocks, axis=0)
    return (cat(r2).astype(BF16), cat(e2).astype(BF16), cat(ja), cat(cf)), tie.astype(jnp.int32)


def _route_kernel(ht_ref, wqt_ref, keys_ref, r2_ref, e2_ref, ja_ref, cf_ref, qt_scr):
    nk = PEER_N_KEYS
    tm = ht_ref.shape[1]
    qt_scr[...] = jnp.dot(wqt_ref[...], ht_ref[...], preferred_element_type=F32).astype(BF16)

    def head(h, carry):
        q0 = pl.multiple_of(h * 2 * nk, 2 * nk)
        s1 = jnp.dot(keys_ref[0], qt_scr[pl.ds(q0, nk), :], preferred_element_type=F32)
        s2 = jnp.dot(keys_ref[1], qt_scr[pl.ds(q0 + nk, nk), :], preferred_element_type=F32)
        def store(cols, res):
            r2_ref[h, :, cols], e2_ref[h, :, cols], ja_ref[h, :, cols], cf_ref[h, :, cols] = res

        for g in range(tm // LANES):
            cols = slice(g * LANES, (g + 1) * LANES)
            res, tie = _route_sorted(s1[:, cols], s2[:, cols])
            store(cols, res)

            @pl.when(jnp.max(tie) > 0)
            def _(cols=cols):
                store(cols, _route_exact(s1[:, cols], s2[:, cols]))
        return carry

    lax.fori_loop(0, PEER_HEADS, head, 0)


def _peer_route(ht, wqt, keys_bf, tm=512):
    d, t = ht.shape
    shp = lambda dt: jax.ShapeDtypeStruct((PEER_HEADS, PEER_N_KEYS, t), dt)
    spec = pl.BlockSpec((PEER_HEADS, PEER_N_KEYS, tm), lambda i: (0, 0, i))
    return pl.pallas_call(
        _route_kernel,
        grid=(t // tm,),
        in_specs=[pl.BlockSpec((d, tm), lambda i: (0, i)),
                  pl.BlockSpec(wqt.shape, lambda i: (0, 0)),
                  pl.BlockSpec(keys_bf.shape, lambda i: (0, 0, 0))],
        out_specs=[spec] * 4,
        out_shape=[shp(BF16), shp(BF16), shp(F32), shp(F32)],
        scratch_shapes=[pltpu.VMEM((wqt.shape[0], tm), BF16)],
        compiler_params=_cparams("parallel"),
        name="peer_route",
    )(ht, wqt, keys_bf)


def _transpose_cast_kernel(x_ref, o_ref):
    o_ref[...] = x_ref[...].T.astype(BF16)


def _transposed_bf16(w_all, layer, tr=512):
    _, r, c = w_all.shape
    return pl.pallas_call(
        _transpose_cast_kernel,
        grid=(r // tr,),
        in_specs=[pl.BlockSpec((None, tr, c), lambda i: (layer, i, 0))],
        out_specs=pl.BlockSpec((c, tr), lambda i: (0, i)),
        out_shape=jax.ShapeDtypeStruct((c, r), BF16),
        compiler_params=_cparams("parallel"),
        name="transpose_cast",
    )(w_all)


EXP_CHUNK = 512
GATE_LANES = 256

def _gated_hidden(gel, a0, r2_ref, e2_ref, ja_ref, cf_ref, w_ref):
    nk = PEER_N_KEYS
    na = gel.shape[0] // nk
    zero = jnp.zeros((), BF16)
    tm = gel.shape[1]

    def row(ref, h, aa):
        return ref[h, pl.ds(a0 + aa, 1), :].astype(BF16)

    ja_rows = [[row(ja_ref, h, aa) for aa in range(na)] for h in range(PEER_HEADS)]
    cf_rows = [[row(cf_ref, h, aa) for aa in range(na)] for h in range(PEER_HEADS)]
    for lg in range(tm // GATE_LANES):
        lanes = slice(lg * GATE_LANES, (lg + 1) * GATE_LANES)
        acc = [None] * na
        for h in range(PEER_HEADS):
            r2c, e2c = r2_ref[h, :, lanes], e2_ref[h, :, lanes]
            for aa in range(na):
                term = jnp.where(r2c < ja_rows[h][aa][:, lanes], e2c * cf_rows[h][aa][:, lanes], zero)
                acc[aa] = term if acc[aa] is None else acc[aa] + term
        for aa in range(na):
            rows = slice(aa * nk, (aa + 1) * nk)
            w_ref[rows, lanes] = acc[aa] * gel[rows, lanes]


def _experts_kernel(ht_ref, u_ref, vt_ref, r2_ref, e2_ref, ja_ref, cf_ref, yt_ref, w_scr):
    j = pl.program_id(1)
    te = u_ref.shape[0]

    @pl.when(j == 0)
    def _():
        yt_ref[...] = jnp.zeros_like(yt_ref)

    ht = ht_ref[...]
    nchunk = te // EXP_CHUNK

    def hidden(c):
        return jnp.dot(u_ref[c * EXP_CHUNK:(c + 1) * EXP_CHUNK, :], ht, preferred_element_type=F32)

    act = hidden(0)
    for c in range(nchunk):
        nxt = hidden(c + 1) if c + 1 < nchunk else None
        gel = (0.5 * act * (1.0 + lax.erf(act * (2.0 ** -0.5)))).astype(BF16)
        a0 = j * (te // PEER_N_KEYS) + c * (EXP_CHUNK // PEER_N_KEYS)
        _gated_hidden(gel, a0, r2_ref, e2_ref, ja_ref, cf_ref, w_scr.at[c])
        yt_ref[...] += jnp.dot(vt_ref[:, c * EXP_CHUNK:(c + 1) * EXP_CHUNK], w_scr[c],
                               preferred_element_type=F32)
        act = nxt


def _peer_experts(ht, u_bf, vt_bf, r2, e2, ja, cf, tm=512, te=1024):
    d, t = ht.shape
    ne = u_bf.shape[0]
    side = pl.BlockSpec((PEER_HEADS, PEER_N_KEYS, tm), lambda i, j: (0, 0, i))
    return pl.pallas_call(
        _experts_kernel,
        grid=(t // tm, ne // te),
        in_specs=[pl.BlockSpec((d, tm), lambda i, j: (0, i)),
                  pl.BlockSpec((te, d), lambda i, j: (j, 0)),
                  pl.BlockSpec((d, te), lambda i, j: (0, j)),
                  side, side, side, side],
        out_specs=pl.BlockSpec((d, tm), lambda i, j: (0, i)),
        out_shape=jax.ShapeDtypeStruct((d, t), F32),
        scratch_shapes=[pltpu.VMEM((te // EXP_CHUNK, EXP_CHUNK, tm), BF16)],
        compiler_params=_cparams("parallel", "arbitrary"),
        name="peer_experts",
    )(ht, u_bf, vt_bf, r2, e2, ja, cf)


def _resid_ln_kernel(x_ref, yt_ref, gf_ref, lg_ref, lb_ref, o_ref, *, alpha):
    o_ref[...] = _layer_norm(alpha * x_ref[...] + (1.0 + gf_ref[0]) * yt_ref[...].T, lg_ref[...], lb_ref[...])


def _resid_ln(x2d, yt, mod3, ln_g, ln_b, seq, alpha, tm=512):
    t, d = x2d.shape
    per_b = seq // tm
    row = pl.BlockSpec((tm, d), lambda i: (i, 0))
    vec = pl.BlockSpec((1, d), lambda i: (0, 0))
    return pl.pallas_call(
        functools.partial(_resid_ln_kernel, alpha=alpha),
        grid=(t // tm,),
        in_specs=[row, pl.BlockSpec((d, tm), lambda i: (0, i)),
                  pl.BlockSpec((1, 1, d), lambda i: (i // per_b, 0, 5)), vec, vec],
        out_specs=row,
        out_shape=jax.ShapeDtypeStruct((t, d), F32),
        compiler_params=_cparams("parallel"),
        name="resid_ln",
    )(x2d, yt, mod3, ln_g.reshape(1, d), ln_b.reshape(1, d))


def kernel(x, c, w_mod, b_mod, w_in, a_q_gain, a_k_gain, c_sink, w_pa, w_pb, w_pc, w_o,
           ln1_g, ln1_b, peer_wq, peer_keys, peer_u, peer_v, ln2_g, ln2_b):
    nb, seq, d = x.shape
    depth = w_mod.shape[0]
    alpha = float((2 * depth) ** 0.25)
    ctab, stab = _rope_tables(seq)
    mod = _modulation(c, w_mod, b_mod)
    x2d = x.reshape(nb * seq, d)
    for l in range(depth):
        mod3 = mod[l].reshape(nb, 1, 6 * d)
        qkv_a, qkv_b, qkv_c, gl = _in_projection(x2d, mod3, w_in, l, seq)
        oa = _mixer_a(qkv_a, ctab, stab, a_q_gain[l], a_k_gain[l], nb, seq)
        ob = _mixer_b(qkv_b, nb, seq)
        oc = _mixer_c(qkv_c, c_sink[l], nb, seq)
        x1, ht = _merge_project(oa, ob, oc, gl, x2d, mod3, ln1_g[l], ln1_b[l],
                                w_pa[l].astype(BF16), w_pb[l].astype(BF16), w_pc[l].astype(BF16),
                                w_o[l].astype(BF16), seq, alpha)
        r2, e2, ja, cf = _peer_route(ht, _transposed_bf16(peer_wq, l), peer_keys[l].astype(BF16))
        yt = _peer_experts(ht, peer_u[l].astype(BF16), _transposed_bf16(peer_v, l), r2, e2, ja, cf)
        x2d = _resid_ln(x1, yt, mod3, ln2_g[l], ln2_b[l], seq, alpha)
    return x2d.reshape(nb, seq, d)
```

```python
import functools

import numpy as np
import jax
import jax.numpy as jnp
from jax import lax
from jax.experimental import pallas as pl
from jax.experimental.pallas import tpu as pltpu

F32 = jnp.float32
BF16 = jnp.bfloat16

HEAD_DIM = 128
A_Q_HEADS, A_KV_HEADS = 8, 2
B_GROUPS = ((128, 1), (512, 4), (2048, 16))
B_HEADS_PER_GROUP = 2
B_HEADS = B_HEADS_PER_GROUP * len(B_GROUPS)
C_Q_HEADS, C_KV_HEADS = 8, 2
C_HALF_WINDOW = 128
N_BRANCHES = 3
GRID_W = 64
ROPE_THETA = 10000.0
PEER_HEADS = 8
PEER_N_KEYS = 128
PEER_TOPK = 16
LN_EPS = 1e-5
QK_EPS = 1e-6
NEG_INF = -1e30

LANES = 128
SUBLANES = 8
BF16_ROWS = 16
VMEM_LIMIT = 56 * 1024 * 1024
MERGE_VMEM_LIMIT = 58 * 1024 * 1024


def _cparams(*sem, vmem=VMEM_LIMIT):
    return pltpu.CompilerParams(dimension_semantics=sem, vmem_limit_bytes=vmem)


def _mod_kernel(cb_ref, w_ref, b_ref, o_ref):
    nb, tn = cb_ref.shape[0], w_ref.shape[2]
    rows = []
    for b in range(nb):
        cb = cb_ref[b]
        pieces = [jnp.sum(w_ref[0, :, n * LANES:(n + 1) * LANES] * cb, axis=0, keepdims=True)
                  for n in range(tn // LANES)]
        rows.append(jnp.concatenate(pieces, axis=1))
    o_ref[0] = jnp.concatenate(rows, axis=0) + b_ref[0]


def _modulation(c, w_mod, b_mod, tn=1024):
    nl, d, n6 = w_mod.shape
    nb = c.shape[0]
    cb = jnp.broadcast_to(c[:, :, None], (nb, d, LANES))
    return pl.pallas_call(
        _mod_kernel,
        grid=(nl, n6 // tn),
        in_specs=[pl.BlockSpec((nb, d, LANES), lambda l, j: (0, 0, 0)),
                  pl.BlockSpec((1, d, tn), lambda l, j: (l, 0, j)),
                  pl.BlockSpec((1, 1, tn), lambda l, j: (l, 0, j))],
        out_specs=pl.BlockSpec((1, nb, tn), lambda l, j: (l, 0, j)),
        out_shape=jax.ShapeDtypeStruct((nl, nb, n6), F32),
        compiler_params=_cparams("parallel", "parallel"),
        name="adaln_modulation",
    )(cb, w_mod, b_mod.reshape(nl, 1, n6))


IN_TN = 768
IN_TILES = (2, 3, 2, 8)


def _inproj_kernel(x_ref, sc_ref, sh_ref, w_ref, oa_ref, ob_ref, oc_ref, og_ref, h_scr):
    j = pl.program_id(1)

    @pl.when(j == 0)
    def _():
        h_scr[...] = (x_ref[...] * (1.0 + sc_ref[0]) + sh_ref[0]).astype(BF16)

    lo = 0
    for ref, n in zip((oa_ref, ob_ref, oc_ref, og_ref), IN_TILES):
        @pl.when((j >= lo) & (j < lo + n))
        def _(ref=ref):
            ref[...] = jnp.dot(h_scr[...], w_ref[...].astype(BF16), preferred_element_type=F32).astype(BF16)
        lo += n


def _in_projection(x2d, mod3, w_all, layer, seq, tm=1024):
    t, d = x2d.shape
    tm = min(tm, seq)
    per_b = seq // tm
    starts = np.cumsum((0,) + IN_TILES[:-1]).tolist()

    def out_spec(lo, n):
        return pl.BlockSpec((tm, IN_TN), lambda i, j: (i, jnp.clip(j - lo, 0, n - 1)))

    return pl.pallas_call(
        _inproj_kernel,
        grid=(t // tm, sum(IN_TILES)),
        in_specs=[pl.BlockSpec((tm, d), lambda i, j: (i, 0)),
                  pl.BlockSpec((1, 1, d), lambda i, j: (i // per_b, 0, 1)),
                  pl.BlockSpec((1, 1, d), lambda i, j: (i // per_b, 0, 0)),
                  pl.BlockSpec((None, d, IN_TN), lambda i, j: (layer, 0, j))],
        out_specs=[out_spec(lo, n) for lo, n in zip(starts, IN_TILES)],
        out_shape=[jax.ShapeDtypeStruct((t, n * IN_TN), BF16) for n in IN_TILES],
        scratch_shapes=[pltpu.VMEM((tm, d), BF16)],
        compiler_params=_cparams("parallel", "arbitrary"),
        name="in_projection",
    )(x2d, mod3, mod3, w_all)


def _rope_tables(seq):
    rows = seq // GRID_W
    row = jnp.repeat(jnp.arange(rows, dtype=F32), GRID_W)
    col = jnp.tile(jnp.arange(GRID_W, dtype=F32), rows)
    quarter = HEAD_DIM // 4
    inv = ROPE_THETA ** (-jnp.arange(quarter, dtype=F32) / quarter)
    ar, ac = row[:, None] * inv, col[:, None] * inv
    ctab = jnp.concatenate([jnp.cos(ar), jnp.cos(ar), jnp.cos(ac), jnp.cos(ac)], axis=1)
    stab = jnp.concatenate([-jnp.sin(ar), jnp.sin(ar), -jnp.sin(ac), jnp.sin(ac)], axis=1)
    return ctab, stab


def _norm_rope(xf, gain, ctab, stab):
    hd = xf.shape[1]
    r = lax.broadcasted_iota(jnp.int32, (hd, hd), 0)
    c = lax.broadcasted_iota(jnp.int32, (hd, hd), 1)
    mean_mat = jnp.full((hd, hd), 1.0 / hd, BF16)
    swap_mat = jnp.where(r == (c ^ (hd // 4)), 1.0, 0.0).astype(BF16)
    ms = jnp.dot((xf * xf).astype(BF16), mean_mat, preferred_element_type=F32)
    xn = xf * lax.rsqrt(ms + QK_EPS) * gain
    partner = jnp.dot(xn.astype(BF16), swap_mat, preferred_element_type=F32)
    return xn * ctab + partner * stab


def _attn_a_kernel(q_ref, k_ref, v_ref, cq_ref, sq_ref, ck_ref, sk_ref, qg_ref, kg_ref,
                   o_ref, kp_scr, vp_scr, *, kc):
    tq, seq = q_ref.shape[0], k_ref.shape[0]
    grp = q_ref.shape[1] // HEAD_DIM

    @pl.when(pl.program_id(2) == 0)
    def _():
        kp_scr[...] = _norm_rope(k_ref[...].astype(F32), kg_ref[...], ck_ref[...], sk_ref[...]).astype(BF16)
        vp_scr[:, :HEAD_DIM] = v_ref[...]
        vp_scr[:, HEAD_DIM:] = jnp.ones((seq, HEAD_DIM), BF16)

    cq, sq, qg = cq_ref[...], sq_ref[...], qg_ref[...]
    scale = HEAD_DIM ** -0.5
    q4 = jnp.concatenate(
        [(_norm_rope(q_ref[:, h * HEAD_DIM:(h + 1) * HEAD_DIM].astype(F32), qg, cq, sq) * scale).astype(BF16)
         for h in range(grp)], axis=0)

    def scores(c):
        return lax.dot_general(q4, kp_scr[c * kc:(c + 1) * kc, :], (((1,), (1,)), ((), ())),
                               preferred_element_type=F32)

    nchunk = seq // kc
    s = scores(0)
    m = acc = None
    for c in range(nchunk):
        nxt = scores(c + 1) if c + 1 < nchunk else None
        smax = jnp.max(s, axis=-1, keepdims=True)
        mn = smax if m is None else jnp.maximum(m, smax)
        p = jnp.exp(s - mn).astype(BF16)
        pv = jnp.dot(p, vp_scr[c * kc:(c + 1) * kc, :], preferred_element_type=F32)
        acc = pv if m is None else jnp.exp(m - mn) * acc + pv
        m, s = mn, nxt
    o = acc[:, :HEAD_DIM] / acc[:, HEAD_DIM:HEAD_DIM + 1]
    for h in range(grp):
        o_ref[:, h * HEAD_DIM:(h + 1) * HEAD_DIM] = o[h * tq:(h + 1) * tq].astype(BF16)


def _mixer_a(qkv, ctab, stab, q_gain, k_gain, nb, seq, tq=256, kc=512):
    t = qkv.shape[0]
    nq = seq // tq
    gw = (A_Q_HEADS // A_KV_HEADS) * HEAD_DIM
    k_blk = A_Q_HEADS
    v_blk = A_Q_HEADS + A_KV_HEADS
    return pl.pallas_call(
        functools.partial(_attn_a_kernel, kc=kc),
        grid=(nb, A_KV_HEADS, nq),
        in_specs=[pl.BlockSpec((tq, gw), lambda b, g, i: (b * nq + i, g)),
                  pl.BlockSpec((seq, HEAD_DIM), lambda b, g, i: (b, k_blk + g)),
                  pl.BlockSpec((seq, HEAD_DIM), lambda b, g, i: (b, v_blk + g)),
                  pl.BlockSpec((tq, HEAD_DIM), lambda b, g, i: (i, 0)),
                  pl.BlockSpec((tq, HEAD_DIM), lambda b, g, i: (i, 0)),
                  pl.BlockSpec((seq, HEAD_DIM), lambda b, g, i: (0, 0)),
                  pl.BlockSpec((seq, HEAD_DIM), lambda b, g, i: (0, 0)),
                  pl.BlockSpec((1, HEAD_DIM), lambda b, g, i: (0, 0)),
                  pl.BlockSpec((1, HEAD_DIM), lambda b, g, i: (0, 0))],
        out_specs=pl.BlockSpec((tq, gw), lambda b, g, i: (b * nq + i, g)),
        out_shape=jax.ShapeDtypeStruct((t, A_Q_HEADS * HEAD_DIM), BF16),
        scratch_shapes=[pltpu.VMEM((seq, HEAD_DIM), BF16), pltpu.VMEM((seq, 2 * HEAD_DIM), BF16)],
        compiler_params=_cparams("parallel", "parallel", "arbitrary"),
        name="mixer_a_axial_gqa",
    )(qkv, qkv, qkv, ctab, stab, ctab, stab, q_gain.reshape(1, HEAD_DIM), k_gain.reshape(1, HEAD_DIM))


def _window(q0, tq, half, seq):
    span = min(tq + 2 * half, seq)
    start = jnp.clip(q0 - half, 0, seq - span)
    return pl.multiple_of(start, 64), span


def _attn_b_kernel(q_ref, k_ref, v_ref, o_ref):
    tq, seq = q_ref.shape[0], k_ref.shape[0]
    q0 = pl.program_id(1) * tq
    scale = HEAD_DIM ** -0.5
    outs, lses = [], []
    for g, (win, r) in enumerate(B_GROUPS):
        half = win // 2
        start, span = _window(q0, tq, half, seq)
        qpos = q0 + lax.broadcasted_iota(jnp.int32, (tq, span), 0)
        kpos = start + lax.broadcasted_iota(jnp.int32, (tq, span), 1)
        dd = qpos - kpos
        ad = jnp.abs(dd)
        valid = (ad <= half) & ((dd & (r - 1)) == 0)
        adf = ad.astype(F32)
        for hh in range(B_HEADS_PER_GROUP):
            head = g * B_HEADS_PER_GROUP + hh
            cols = slice(head * HEAD_DIM, (head + 1) * HEAD_DIM)
            slope = float(2.0 ** (-8.0 * (head + 1) / B_HEADS))
            s = lax.dot_general(q_ref[:, cols], k_ref[pl.ds(start, span), cols],
                                (((1,), (1,)), ((), ())), preferred_element_type=F32) * scale
            s = jnp.where(valid, s - slope * adf, NEG_INF)
            m = jnp.max(s, axis=-1, keepdims=True)
            p = jnp.exp(s - m)
            l = jnp.sum(p, axis=-1, keepdims=True)
            o = jnp.dot(p.astype(BF16), v_ref[pl.ds(start, span), cols], preferred_element_type=F32)
            outs.append(o / l)
            lses.append(m + jnp.log(l))
    ng = len(B_GROUPS)
    for hh in range(B_HEADS_PER_GROUP):
        hl = [lses[g * B_HEADS_PER_GROUP + hh] for g in range(ng)]
        mx = functools.reduce(jnp.maximum, hl)
        ex = [jnp.exp(x - mx) for x in hl]
        tot = functools.reduce(lambda a, b: a + b, ex)
        for g in range(ng):
            head = g * B_HEADS_PER_GROUP + hh
            o_ref[:, head * HEAD_DIM:(head + 1) * HEAD_DIM] = (outs[head] * (ex[g] / tot)).astype(BF16)


def _mixer_b(qkv, nb, seq, tq=256):
    t = qkv.shape[0]
    nq = seq // tq
    bw = B_HEADS * HEAD_DIM
    return pl.pallas_call(
        _attn_b_kernel,
        grid=(nb, nq),
        in_specs=[pl.BlockSpec((tq, bw), lambda b, i: (b * nq + i, 0)),
                  pl.BlockSpec((seq, bw), lambda b, i: (b, 1)),
                  pl.BlockSpec((seq, bw), lambda b, i: (b, 2))],
        out_specs=pl.BlockSpec((tq, bw), lambda b, i: (b * nq + i, 0)),
        out_shape=jax.ShapeDtypeStruct((t, bw), BF16),
        compiler_params=_cparams("parallel", "arbitrary"),
        name="mixer_b_dilated",
    )(qkv, qkv, qkv)


def _attn_c_kernel(hp_ref, q_ref, k_ref, v_ref, o_ref):
    tq, seq = q_ref.shape[0], k_ref.shape[0]
    grp = q_ref.shape[1] // HEAD_DIM
    g = pl.program_id(1)
    q0 = pl.program_id(2) * tq
    start, span = _window(q0, tq, C_HALF_WINDOW, seq)
    qpos = q0 + lax.broadcasted_iota(jnp.int32, (tq, span), 0)
    kpos = start + lax.broadcasted_iota(jnp.int32, (tq, span), 1)
    ad = jnp.abs(qpos - kpos)
    valid = ad <= C_HALF_WINDOW
    adf = ad.astype(F32)
    kk = k_ref[pl.ds(start, span), :]
    vv = v_ref[pl.ds(start, span), :]
    scale = HEAD_DIM ** -0.5
    for hh in range(grp):
        cols = slice(hh * HEAD_DIM, (hh + 1) * HEAD_DIM)
        slope, sink = hp_ref[g, hh], hp_ref[g, grp + hh]
        s = lax.dot_general(q_ref[:, cols], kk, (((1,), (1,)), ((), ())),
                            preferred_element_type=F32) * scale
        s = jnp.where(valid, s - slope * adf, NEG_INF)
        m = jnp.maximum(jnp.max(s, axis=-1, keepdims=True), sink)
        p = jnp.exp(s - m)
        l = jnp.sum(p, axis=-1, keepdims=True) + jnp.exp(sink - m)
        o = jnp.dot(p.astype(BF16), vv, preferred_element_type=F32)
        o_ref[:, cols] = (o / l).astype(BF16)


def _mixer_c(qkv, sink, nb, seq, tq=256):
    t = qkv.shape[0]
    nq = seq // tq
    grp = C_Q_HEADS // C_KV_HEADS
    gw = grp * HEAD_DIM
    slopes = (2.0 ** (-8.0 * jnp.arange(1, C_Q_HEADS + 1, dtype=F32) / C_Q_HEADS)).reshape(C_KV_HEADS, grp)
    head_params = jnp.concatenate([slopes, sink.astype(F32)], axis=1)
    return pl.pallas_call(
        _attn_c_kernel,
        grid=(nb, C_KV_HEADS, nq),
        in_specs=[pl.BlockSpec(memory_space=pltpu.SMEM),
                  pl.BlockSpec((tq, gw), lambda b, g, i: (b * nq + i, g)),
                  pl.BlockSpec((seq, HEAD_DIM), lambda b, g, i: (b, C_Q_HEADS + g)),
                  pl.BlockSpec((seq, HEAD_DIM), lambda b, g, i: (b, C_Q_HEADS + C_KV_HEADS + g))],
        out_specs=pl.BlockSpec((tq, gw), lambda b, g, i: (b * nq + i, g)),
        out_shape=jax.ShapeDtypeStruct((t, C_Q_HEADS * HEAD_DIM), BF16),
        compiler_params=_cparams("parallel", "parallel", "arbitrary"),
        name="mixer_c_window_sink",
    )(head_params, qkv, qkv, qkv)


def _layer_norm(z, g, b):
    mu = jnp.mean(z, axis=-1, keepdims=True)
    zc = z - mu
    var = jnp.mean(zc * zc, axis=-1, keepdims=True)
    return zc * lax.rsqrt(var + LN_EPS) * g + b


def _branch_kernel(oa_ref, ob_ref, oc_ref, gl_ref, wa_ref, wb_ref, wc_ref, m_ref):
    d = m_ref.shape[1]
    merged = None
    for i, (o_ref, w_ref) in enumerate(((oa_ref, wa_ref), (ob_ref, wb_ref), (oc_ref, wc_ref))):
        y = jnp.dot(o_ref[...], w_ref[...], preferred_element_type=F32)
        term = jax.nn.sigmoid(gl_ref[:, i * d:(i + 1) * d].astype(F32)) * y
        merged = term if merged is None else merged + term
    m_ref[...] = merged.astype(BF16)


def _outproj_kernel(m_ref, x_ref, ga_ref, scf_ref, shf_ref, lg_ref, lb_ref, wo_ref, x1_ref, ht_ref, *, alpha):
    y = jnp.dot(m_ref[...], wo_ref[...], preferred_element_type=F32)
    x1 = _layer_norm(alpha * x_ref[...] + (1.0 + ga_ref[0]) * y, lg_ref[...], lb_ref[...])
    x1_ref[...] = x1
    h2 = x1 * (1.0 + scf_ref[0]) + shf_ref[0]
    ht_ref[...] = h2.T.astype(BF16)


def _merge_project(oa, ob, oc, gl, x2d, mod3, ln_g, ln_b, wa, wb, wc, wo, seq, alpha, tm=512):
    t, d = x2d.shape
    per_b = seq // tm
    row = lambda w: pl.BlockSpec((tm, w), lambda i: (i, 0))
    whole = lambda a: pl.BlockSpec(a.shape, lambda i: (0, 0), pipeline_mode=pl.Buffered(1))
    modspec = lambda k: pl.BlockSpec((1, 1, d), lambda i: (i // per_b, 0, k))
    vec = pl.BlockSpec((1, d), lambda i: (0, 0))
    merged = pl.pallas_call(
        _branch_kernel,
        grid=(t // tm,),
        in_specs=[row(oa.shape[1]), row(ob.shape[1]), row(oc.shape[1]), row(gl.shape[1]),
                  whole(wa), whole(wb), whole(wc)],
        out_specs=row(d),
        out_shape=jax.ShapeDtypeStruct((t, d), BF16),
        compiler_params=_cparams("parallel"),
        name="branch_merge",
    )(oa, ob, oc, gl, wa, wb, wc)
    return pl.pallas_call(
        functools.partial(_outproj_kernel, alpha=alpha),
        grid=(t // tm,),
        in_specs=[row(d), row(d), modspec(2), modspec(4), modspec(3), vec, vec, whole(wo)],
        out_specs=[pl.BlockSpec((tm, d), lambda i: (i, 0)), pl.BlockSpec((d, tm), lambda i: (0, i))],
        out_shape=[jax.ShapeDtypeStruct((t, d), F32), jax.ShapeDtypeStruct((d, t), BF16)],
        compiler_params=_cparams("parallel"),
        name="outproj_ln",
    )(merged, x2d, mod3, mod3, mod3, ln_g.reshape(1, d), ln_b.reshape(1, d), wo)


CAND_ROWS = 4


def _top_exact(s, n, order):
    nrow = lax.broadcasted_iota(jnp.int32, (n, s.shape[1]), 0)

    def step(i, carry):
        s, rank, vals = carry
        m = jnp.max(s, axis=0, keepdims=True)
        first = jnp.min(jnp.where(s == m, order, jnp.iinfo(jnp.int32).max), axis=0, keepdims=True)
        hit = order == first
        return jnp.where(hit, -jnp.inf, s), jnp.where(hit, i, rank), jnp.where(nrow == i, m, vals)

    _, rank, vals = lax.fori_loop(0, n, step, (s, jnp.full(s.shape, n, jnp.int32), jnp.zeros(nrow.shape, F32)))
    return vals, rank


def _candidates(v1, v2):
    k, c = PEER_TOPK, CAND_ROWS
    ki = lax.broadcasted_iota(jnp.int32, v1.shape, 0)
    v1_tail = jnp.where(ki >= c, v1, -jnp.inf)
    cand = jnp.concatenate([v1[i:i + 1] + v2 for i in range(c)] + [v2[j:j + 1] + v1_tail for j in range(c)], axis=0)
    flat = jnp.concatenate([ki + i * k for i in range(c)] +
                           [jnp.where(ki >= c, ki * k + j, -1) for j in range(c)], axis=0)
    return cand, flat


def _route_outputs(s1, s2, v1, v2, rank2, sel, is_ith):
    k, c = PEER_TOPK, CAND_ROWS
    e1 = jnp.exp(v1 - v1[0:1])
    e2 = jnp.exp(v2 - v2[0:1])
    prod = jnp.concatenate([e1[i:i + 1] * e2 for i in range(c)] + [e2[j:j + 1] * e1 for j in range(c)], axis=0)
    z = jnp.sum(jnp.where(sel, prod, 0.0), axis=0, keepdims=True)
    self32 = sel.astype(F32)
    tail = functools.reduce(lambda a, b: a + b, [self32[(c + j) * k:(c + j + 1) * k] for j in range(c)])
    ja = jnp.zeros(s1.shape, F32)
    for i in range(k):
        cnt = jnp.sum(self32[i * k:(i + 1) * k], axis=0, keepdims=True) if i < c else tail[i:i + 1]
        ja = jnp.where(is_ith(i), cnt, ja)
    return (rank2.astype(F32).astype(BF16), jnp.exp(s2 - v2[0:1]).astype(BF16), ja,
            jnp.exp(s1 - v1[0:1]) / z)


def _route_exact(s1, s2):
    k = PEER_TOPK
    rows = lax.broadcasted_iota(jnp.int32, s1.shape, 0)
    v1, rank1 = _top_exact(s1, k, rows)
    v2, rank2 = _top_exact(s2, k, rows)
    cand, flat = _candidates(v1, v2)
    _, crank = _top_exact(cand, k, flat)
    return _route_outputs(s1, s2, v1, v2, rank2, crank < k, lambda i: rank1 == i)


def _sort_network(n):
    def merge(lo, hi, r):
        step = r * 2
        if step < hi - lo:
            yield from merge(lo, hi, step)
            yield from merge(lo + r, hi, step)
            yield from [(i, i + r) for i in range(lo + r, hi - r, step)]
        else:
            yield (lo, lo + r)

    def sort(lo, hi):
        if hi - lo >= 1:
            mid = lo + (hi - lo) // 2
            yield from sort(lo, mid)
            yield from sort(mid + 1, hi)
            yield from merge(lo, hi, 1)

    return tuple(sort(0, n - 1))


def _rep_max(x):
    return jnp.broadcast_to(jnp.max(x, axis=0, keepdims=True), x.shape)


def _rep_sum(x):
    return jnp.broadcast_to(jnp.sum(x, axis=0, keepdims=True), x.shape)


def _top_sorted(groups, n):
    work = [list(g) for g in groups]
    for i, j in _sort_network(len(work[0])):
        for v in work:
            v[i], v[j] = jnp.maximum(v[i], v[j]), jnp.minimum(v[i], v[j])
    vals = [[] for _ in work]
    for it in range(n):
        for v, out in zip(work, vals):
            m = _rep_max(v[0])
            out.append(m)
            hit = v[0] == m
            for k in range(n - 1 - it):
                v[k] = jnp.where(hit, v[k + 1], v[k])
    return vals


def _route_sorted(s1, s2):
    k, c, sub = PEER_TOPK, CAND_ROWS, SUBLANES
    nblk = s1.shape[0] // sub
    b1 = [s1[sub * q:sub * (q + 1)] for q in range(nblk)]
    b2 = [s2[sub * q:sub * (q + 1)] for q in range(nblk)]
    v1, v2 = _top_sorted([b1, b2], k)
    srow = lax.broadcasted_iota(jnp.int32, b1[0].shape, 0)

    def stack(vals):
        return [functools.reduce(lambda acc, t: jnp.where(srow == t, vals[blk * sub + t], acc),
                                 range(1, sub), vals[blk * sub]) for blk in range(k // sub)]

    st1, st2 = stack(v1), stack(v2)
    st1_tail = [jnp.where(srow >= c, st1[0], -jnp.inf)] + st1[1:]
    cand = ([v1[i] + st2[b] for i in range(c) for b in range(k // sub)] +
            [v2[j] + st1_tail[b] for j in range(c) for b in range(k // sub)])
    cv = _top_sorted([cand], k)[0]
    sel = [x >= cv[k - 1] for x in cand]
    e1s = [jnp.exp(x - v1[0]) for x in st1]
    e2s = [jnp.exp(x - v2[0]) for x in st2]
    prod = ([jnp.exp(v1[i] - v1[0]) * e2s[b] for i in range(c) for b in range(k // sub)] +
            [jnp.exp(v2[j] - v2[0]) * e1s[b] for j in range(c) for b in range(k // sub)])
    z = _rep_sum(functools.reduce(lambda a, b: a + b, [jnp.where(s, p, 0.0) for s, p in zip(sel, prod)]))
    ones = [jnp.where(s, 1.0, 0.0) for s in sel]
    per = k // sub
    tail = [functools.reduce(lambda a, b: a + b, [ones[(c + j) * per + b] for j in range(c)]) for b in range(per)]
    counts = []
    for i in range(k):
        if i < c:
            counts.append(_rep_sum(functools.reduce(lambda a, b: a + b, ones[i * per:(i + 1) * per])))
        else:
            counts.append(_rep_sum(jnp.where(srow == i % sub, tail[i // sub], 0.0)))
    inv_z = 1.0 / z
    ja, cf, r2, e2 = [], [], [], []
    for q in range(nblk):
        jq = jnp.zeros(b1[q].shape, F32)
        rq = jnp.zeros(b2[q].shape, F32)
        for i in range(k):
            jq = jnp.where(b1[q] == v1[i], counts[i], jq)
            rq = jnp.where(b2[q] < v2[i], float(i + 1), rq)
        ja.append(jq)
        r2.append(rq)
        cf.append(jnp.exp(b1[q] - v1[0]) * inv_z)
        e2.append(jnp.exp(b2[q] - v2[0]))

    def picked(blocks, thr):
        return _rep_sum(functools.reduce(lambda a, b: a + b, [jnp.where(x >= thr, 1.0, 0.0) for x in blocks]))

    def repeats(vals):
        return functools.reduce(lambda a, b: a | b, [vals[i] == vals[i + 1] for i in range(k - 1)])

    tie = ((picked(b1, v1[k - 1]) != k) | (picked(b2, v2[k - 1]) != k) | (picked(cand, cv[k - 1]) != k) |
           repeats(v1) | repeats(v2) | repeats(cv))
    cat = lambda blocks: jnp.concatenate(blocks, axis=0)
    return (cat(r2).astype(BF16), cat(e2).astype(BF16), cat(ja), cat(cf)), tie.astype(jnp.int32)


def _route_kernel(ht_ref, wqt_ref, keys_ref, r2_ref, e2_ref, ja_ref, cf_ref, qt_scr):
    nk = PEER_N_KEYS
    tm = ht_ref.shape[1]
    qt_scr[...] = jnp.dot(wqt_ref[...], ht_ref[...], preferred_element_type=F32).astype(BF16)

    def head(h, carry):
        q0 = pl.multiple_of(h * 2 * nk, 2 * nk)
        s1 = jnp.dot(keys_ref[0], qt_scr[pl.ds(q0, nk), :], preferred_element_type=F32)
        s2 = jnp.dot(keys_ref[1], qt_scr[pl.ds(q0 + nk, nk), :], preferred_element_type=F32)
        def store(cols, res):
            r2_ref[h, :, cols], e2_ref[h, :, cols], ja_ref[h, :, cols], cf_ref[h, :, cols] = res

        for g in range(tm // LANES):
            cols = slice(g * LANES, (g + 1) * LANES)
            res, tie = _route_sorted(s1[:, cols], s2[:, cols])
            store(cols, res)

            @pl.when(jnp.max(tie) > 0)
            def _(cols=cols):
                store(cols, _route_exact(s1[:, cols], s2[:, cols]))
        return carry

    lax.fori_loop(0, PEER_HEADS, head, 0)


def _peer_route(ht, wqt, keys_bf, tm=512):
    d, t = ht.shape
    shp = lambda dt: jax.ShapeDtypeStruct((PEER_HEADS, PEER_N_KEYS, t), dt)
    spec = pl.BlockSpec((PEER_HEADS, PEER_N_KEYS, tm), lambda i: (0, 0, i))
    return pl.pallas_call(
        _route_kernel,
        grid=(t // tm,),
        in_specs=[pl.BlockSpec((d, tm), lambda i: (0, i)),
                  pl.BlockSpec(wqt.shape, lambda i: (0, 0)),
                  pl.BlockSpec(keys_bf.shape, lambda i: (0, 0, 0))],
        out_specs=[spec] * 4,
        out_shape=[shp(BF16), shp(BF16), shp(F32), shp(F32)],
        scratch_shapes=[pltpu.VMEM((wqt.shape[0], tm), BF16)],
        compiler_params=_cparams("parallel"),
        name="peer_route",
    )(ht, wqt, keys_bf)


def _cast_kernel(x_ref, o_ref, *, transpose):
    x = x_ref[...]
    o_ref[...] = (x.T if transpose else x).astype(BF16)


def _layer_bf16(w_all, layer, transpose, tr=512):
    _, r, c = w_all.shape
    return pl.pallas_call(
        functools.partial(_cast_kernel, transpose=transpose),
        grid=(r // tr,),
        in_specs=[pl.BlockSpec((None, tr, c), lambda i: (layer, i, 0))],
        out_specs=pl.BlockSpec((c, tr), lambda i: (0, i)) if transpose else pl.BlockSpec((tr, c), lambda i: (i, 0)),
        out_shape=jax.ShapeDtypeStruct((c, r) if transpose else (r, c), BF16),
        compiler_params=_cparams("parallel"),
        name="transpose_cast" if transpose else "cast",
    )(w_all)


EXP_CHUNK = 512
GATE_LANES = 256

def _gated_hidden(gel, a0, r2_ref, e2_ref, ja_ref, cf_ref, w_ref):
    nk = PEER_N_KEYS
    na = gel.shape[0] // nk
    zero = jnp.zeros((), BF16)
    tm = gel.shape[1]

    def row(ref, h, aa):
        return ref[h, pl.ds(a0 + aa, 1), :].astype(BF16)

    ja_rows = [[row(ja_ref, h, aa) for aa in range(na)] for h in range(PEER_HEADS)]
    cf_rows = [[row(cf_ref, h, aa) for aa in range(na)] for h in range(PEER_HEADS)]
    for lg in range(tm // GATE_LANES):
        lanes = slice(lg * GATE_LANES, (lg + 1) * GATE_LANES)
        acc = [None] * na
        for h in range(PEER_HEADS):
            r2c, e2c = r2_ref[h, :, lanes], e2_ref[h, :, lanes]
            for aa in range(na):
                term = jnp.where(r2c < ja_rows[h][aa][:, lanes], e2c * cf_rows[h][aa][:, lanes], zero)
                acc[aa] = term if acc[aa] is None else acc[aa] + term
        for aa in range(na):
            rows = slice(aa * nk, (aa + 1) * nk)
            w_ref[rows, lanes] = acc[aa] * gel[rows, lanes]


def _experts_kernel(ht_ref, u_ref, vt_ref, r2_ref, e2_ref, ja_ref, cf_ref, yt_ref, w_scr):
    j = pl.program_id(1)
    te = u_ref.shape[0]

    @pl.when(j == 0)
    def _():
        yt_ref[...] = jnp.zeros_like(yt_ref)

    ht = ht_ref[...]
    nchunk = te // EXP_CHUNK

    def hidden(c):
        return jnp.dot(u_ref[c * EXP_CHUNK:(c + 1) * EXP_CHUNK, :], ht, preferred_element_type=F32)

    act = hidden(0)
    for c in range(nchunk):
        nxt = hidden(c + 1) if c + 1 < nchunk else None
        gel = (0.5 * act * (1.0 + lax.erf(act * (2.0 ** -0.5)))).astype(BF16)
        a0 = j * (te // PEER_N_KEYS) + c * (EXP_CHUNK // PEER_N_KEYS)
        _gated_hidden(gel, a0, r2_ref, e2_ref, ja_ref, cf_ref, w_scr.at[c])
        yt_ref[...] += jnp.dot(vt_ref[:, c * EXP_CHUNK:(c + 1) * EXP_CHUNK], w_scr[c],
                               preferred_element_type=F32)
        act = nxt


def _peer_experts(ht, u_bf, vt_bf, r2, e2, ja, cf, tm=512, te=1024):
    d, t = ht.shape
    ne = u_bf.shape[0]
    side = pl.BlockSpec((PEER_HEADS, PEER_N_KEYS, tm), lambda i, j: (0, 0, i))
    return pl.pallas_call(
        _experts_kernel,
        grid=(t // tm, ne // te),
        in_specs=[pl.BlockSpec((d, tm), lambda i, j: (0, i)),
                  pl.BlockSpec((te, d), lambda i, j: (j, 0)),
                  pl.BlockSpec((d, te), lambda i, j: (0, j)),
                  side, side, side, side],
        out_specs=pl.BlockSpec((d, tm), lambda i, j: (0, i)),
        out_shape=jax.ShapeDtypeStruct((d, t), F32),
        scratch_shapes=[pltpu.VMEM((te // EXP_CHUNK, EXP_CHUNK, tm), BF16)],
        compiler_params=_cparams("parallel", "arbitrary"),
        name="peer_experts",
    )(ht, u_bf, vt_bf, r2, e2, ja, cf)


def _resid_ln_kernel(x_ref, yt_ref, gf_ref, lg_ref, lb_ref, o_ref, *, alpha):
    o_ref[...] = _layer_norm(alpha * x_ref[...] + (1.0 + gf_ref[0]) * yt_ref[...].T, lg_ref[...], lb_ref[...])


def _resid_ln(x2d, yt, mod3, ln_g, ln_b, seq, alpha, tm=512):
    t, d = x2d.shape
    per_b = seq // tm
    row = pl.BlockSpec((tm, d), lambda i: (i, 0))
    vec = pl.BlockSpec((1, d), lambda i: (0, 0))
    return pl.pallas_call(
        functools.partial(_resid_ln_kernel, alpha=alpha),
        grid=(t // tm,),
        in_specs=[row, pl.BlockSpec((d, tm), lambda i: (0, i)),
                  pl.BlockSpec((1, 1, d), lambda i: (i // per_b, 0, 5)), vec, vec],
        out_specs=row,
        out_shape=jax.ShapeDtypeStruct((t, d), F32),
        compiler_params=_cparams("parallel"),
        name="resid_ln",
    )(x2d, yt, mod3, ln_g.reshape(1, d), ln_b.reshape(1, d))


def kernel(x, c, w_mod, b_mod, w_in, a_q_gain, a_k_gain, c_sink, w_pa, w_pb, w_pc, w_o,
           ln1_g, ln1_b, peer_wq, peer_keys, peer_u, peer_v, ln2_g, ln2_b):
    nb, seq, d = x.shape
    depth = w_mod.shape[0]
    alpha = float((2 * depth) ** 0.25)
    ctab, stab = _rope_tables(seq)
    mod = _modulation(c, w_mod, b_mod)
    x2d = x.reshape(nb * seq, d)
    for l in range(depth):
        mod3 = mod[l].reshape(nb, 1, 6 * d)
        qkv_a, qkv_b, qkv_c, gl = _in_projection(x2d, mod3, w_in, l, seq)
        oa = _mixer_a(qkv_a, ctab, stab, a_q_gain[l], a_k_gain[l], nb, seq)
        ob = _mixer_b(qkv_b, nb, seq)
        oc = _mixer_c(qkv_c, c_sink[l], nb, seq)
        x1, ht = _merge_project(oa, ob, oc, gl, x2d, mod3, ln1_g[l], ln1_b[l],
                                w_pa[l].astype(BF16), w_pb[l].astype(BF16), w_pc[l].astype(BF16),
                                w_o[l].astype(BF16), seq, alpha)
        r2, e2, ja, cf = _peer_route(ht, _layer_bf16(peer_wq, l, True), peer_keys[l].astype(BF16))
        yt = _peer_experts(ht, _layer_bf16(peer_u, l, False), _layer_bf16(peer_v, l, True), r2, e2, ja, cf)
        x2d = _resid_ln(x1, yt, mod3, ln2_g[l], ln2_b[l], seq, alpha)
    return x2d.reshape(nb, seq, d)
```

```python
import functools

import numpy as np
import jax
import jax.numpy as jnp
from jax import lax
from jax.experimental import pallas as pl
from jax.experimental.pallas import tpu as pltpu

F32 = jnp.float32
BF16 = jnp.bfloat16

HEAD_DIM = 128
A_Q_HEADS, A_KV_HEADS = 8, 2
B_GROUPS = ((128, 1), (512, 4), (2048, 16))
B_HEADS_PER_GROUP = 2
B_HEADS = B_HEADS_PER_GROUP * len(B_GROUPS)
C_Q_HEADS, C_KV_HEADS = 8, 2
C_HALF_WINDOW = 128
N_BRANCHES = 3
GRID_W = 64
ROPE_THETA = 10000.0
PEER_HEADS = 8
PEER_N_KEYS = 128
PEER_TOPK = 16
LN_EPS = 1e-5
QK_EPS = 1e-6
NEG_INF = -1e30

LANES = 128
SUBLANES = 8
BF16_ROWS = 16
VMEM_LIMIT = 56 * 1024 * 1024
MERGE_VMEM_LIMIT = 58 * 1024 * 1024


def _cparams(*sem, vmem=VMEM_LIMIT):
    return pltpu.CompilerParams(dimension_semantics=sem, vmem_limit_bytes=vmem)


def _mod_kernel(cb_ref, w_ref, b_ref, o_ref):
    nb, tn = cb_ref.shape[0], w_ref.shape[2]
    rows = []
    for b in range(nb):
        cb = cb_ref[b]
        pieces = [jnp.sum(w_ref[0, :, n * LANES:(n + 1) * LANES] * cb, axis=0, keepdims=True)
                  for n in range(tn // LANES)]
        rows.append(jnp.concatenate(pieces, axis=1))
    o_ref[0] = jnp.concatenate(rows, axis=0) + b_ref[0]


def _modulation(c, w_mod, b_mod, tn=1024):
    nl, d, n6 = w_mod.shape
    nb = c.shape[0]
    cb = jnp.broadcast_to(c[:, :, None], (nb, d, LANES))
    return pl.pallas_call(
        _mod_kernel,
        grid=(nl, n6 // tn),
        in_specs=[pl.BlockSpec((nb, d, LANES), lambda l, j: (0, 0, 0)),
                  pl.BlockSpec((1, d, tn), lambda l, j: (l, 0, j)),
                  pl.BlockSpec((1, 1, tn), lambda l, j: (l, 0, j))],
        out_specs=pl.BlockSpec((1, nb, tn), lambda l, j: (l, 0, j)),
        out_shape=jax.ShapeDtypeStruct((nl, nb, n6), F32),
        compiler_params=_cparams("parallel", "parallel"),
        name="adaln_modulation",
    )(cb, w_mod, b_mod.reshape(nl, 1, n6))


IN_TN = 768
IN_TILES = (2, 3, 2, 8)


def _inproj_kernel(x_ref, sc_ref, sh_ref, w_ref, oa_ref, ob_ref, oc_ref, og_ref, h_scr):
    j = pl.program_id(1)

    @pl.when(j == 0)
    def _():
        h_scr[...] = (x_ref[...] * (1.0 + sc_ref[0]) + sh_ref[0]).astype(BF16)

    lo = 0
    for ref, n in zip((oa_ref, ob_ref, oc_ref, og_ref), IN_TILES):
        @pl.when((j >= lo) & (j < lo + n))
        def _(ref=ref):
            ref[...] = jnp.dot(h_scr[...], w_ref[...].astype(BF16), preferred_element_type=F32).astype(BF16)
        lo += n


def _in_projection(x2d, mod3, w_all, layer, seq, tm=1024):
    t, d = x2d.shape
    tm = min(tm, seq)
    per_b = seq // tm
    starts = np.cumsum((0,) + IN_TILES[:-1]).tolist()

    def out_spec(lo, n):
        return pl.BlockSpec((tm, IN_TN), lambda i, j: (i, jnp.clip(j - lo, 0, n - 1)))

    return pl.pallas_call(
        _inproj_kernel,
        grid=(t // tm, sum(IN_TILES)),
        in_specs=[pl.BlockSpec((tm, d), lambda i, j: (i, 0)),
                  pl.BlockSpec((1, 1, d), lambda i, j: (i // per_b, 0, 1)),
                  pl.BlockSpec((1, 1, d), lambda i, j: (i // per_b, 0, 0)),
                  pl.BlockSpec((None, d, IN_TN), lambda i, j: (layer, 0, j))],
        out_specs=[out_spec(lo, n) for lo, n in zip(starts, IN_TILES)],
        out_shape=[jax.ShapeDtypeStruct((t, n * IN_TN), BF16) for n in IN_TILES],
        scratch_shapes=[pltpu.VMEM((tm, d), BF16)],
        compiler_params=_cparams("parallel", "arbitrary"),
        name="in_projection",
    )(x2d, mod3, mod3, w_all)


def _rope_tables(seq):
    rows = seq // GRID_W
    row = jnp.repeat(jnp.arange(rows, dtype=F32), GRID_W)
    col = jnp.tile(jnp.arange(GRID_W, dtype=F32), rows)
    quarter = HEAD_DIM // 4
    inv = ROPE_THETA ** (-jnp.arange(quarter, dtype=F32) / quarter)
    ar, ac = row[:, None] * inv, col[:, None] * inv
    ctab = jnp.concatenate([jnp.cos(ar), jnp.cos(ar), jnp.cos(ac), jnp.cos(ac)], axis=1)
    stab = jnp.concatenate([-jnp.sin(ar), jnp.sin(ar), -jnp.sin(ac), jnp.sin(ac)], axis=1)
    return ctab, stab


def _norm_rope(xf, gain, ctab, stab):
    hd = xf.shape[1]
    r = lax.broadcasted_iota(jnp.int32, (hd, hd), 0)
    c = lax.broadcasted_iota(jnp.int32, (hd, hd), 1)
    mean_mat = jnp.full((hd, hd), 1.0 / hd, BF16)
    swap_mat = jnp.where(r == (c ^ (hd // 4)), 1.0, 0.0).astype(BF16)
    ms = jnp.dot((xf * xf).astype(BF16), mean_mat, preferred_element_type=F32)
    xn = xf * lax.rsqrt(ms + QK_EPS) * gain
    partner = jnp.dot(xn.astype(BF16), swap_mat, preferred_element_type=F32)
    return xn * ctab + partner * stab


def _attn_a_kernel(q_ref, k_ref, v_ref, cq_ref, sq_ref, ck_ref, sk_ref, qg_ref, kg_ref, eu_ref, ev_ref,
                   o_ref, ub_ref, vt_ref, kp_scr, vp_scr, *, kc):
    tq, seq = q_ref.shape[0], k_ref.shape[0]
    grp = q_ref.shape[1] // HEAD_DIM
    ub_ref[...] = eu_ref[...].astype(BF16)
    vt_ref[...] = ev_ref[...].T.astype(BF16)

    @pl.when(pl.program_id(2) == 0)
    def _():
        kp_scr[...] = _norm_rope(k_ref[...].astype(F32), kg_ref[...], ck_ref[...], sk_ref[...]).astype(BF16)
        vp_scr[:, :HEAD_DIM] = v_ref[...]
        vp_scr[:, HEAD_DIM:] = jnp.ones((seq, HEAD_DIM), BF16)

    cq, sq, qg = cq_ref[...], sq_ref[...], qg_ref[...]
    scale = HEAD_DIM ** -0.5
    q4 = jnp.concatenate(
        [(_norm_rope(q_ref[:, h * HEAD_DIM:(h + 1) * HEAD_DIM].astype(F32), qg, cq, sq) * scale).astype(BF16)
         for h in range(grp)], axis=0)

    def scores(c):
        return lax.dot_general(q4, kp_scr[c * kc:(c + 1) * kc, :], (((1,), (1,)), ((), ())),
                               preferred_element_type=F32)

    nchunk = seq // kc
    s = scores(0)
    m = acc = None
    for c in range(nchunk):
        nxt = scores(c + 1) if c + 1 < nchunk else None
        smax = jnp.max(s, axis=-1, keepdims=True)
        mn = smax if m is None else jnp.maximum(m, smax)
        p = jnp.exp(s - mn).astype(BF16)
        pv = jnp.dot(p, vp_scr[c * kc:(c + 1) * kc, :], preferred_element_type=F32)
        acc = pv if m is None else jnp.exp(m - mn) * acc + pv
        m, s = mn, nxt
    o = acc[:, :HEAD_DIM] / acc[:, HEAD_DIM:HEAD_DIM + 1]
    for h in range(grp):
        o_ref[:, h * HEAD_DIM:(h + 1) * HEAD_DIM] = o[h * tq:(h + 1) * tq].astype(BF16)


def _mixer_a(qkv, ctab, stab, q_gain, k_gain, peer_u, peer_v, layer, nb, seq, tq=256, kc=512):
    t = qkv.shape[0]
    nq = seq // tq
    gw = (A_Q_HEADS // A_KV_HEADS) * HEAD_DIM
    k_blk = A_Q_HEADS
    v_blk = A_Q_HEADS + A_KV_HEADS
    _, ne, d = peer_u.shape
    slab = ne // (nb * A_KV_HEADS * nq)
    step = lambda b, g, i: (b * A_KV_HEADS + g) * nq + i
    expert_rows = pl.BlockSpec((None, slab, d), lambda b, g, i: (layer, step(b, g, i), 0))
    return pl.pallas_call(
        functools.partial(_attn_a_kernel, kc=kc),
        grid=(nb, A_KV_HEADS, nq),
        in_specs=[pl.BlockSpec((tq, gw), lambda b, g, i: (b * nq + i, g)),
                  pl.BlockSpec((seq, HEAD_DIM), lambda b, g, i: (b, k_blk + g)),
                  pl.BlockSpec((seq, HEAD_DIM), lambda b, g, i: (b, v_blk + g)),
                  pl.BlockSpec((tq, HEAD_DIM), lambda b, g, i: (i, 0)),
                  pl.BlockSpec((tq, HEAD_DIM), lambda b, g, i: (i, 0)),
                  pl.BlockSpec((seq, HEAD_DIM), lambda b, g, i: (0, 0)),
                  pl.BlockSpec((seq, HEAD_DIM), lambda b, g, i: (0, 0)),
                  pl.BlockSpec((1, HEAD_DIM), lambda b, g, i: (0, 0)),
                  pl.BlockSpec((1, HEAD_DIM), lambda b, g, i: (0, 0)),
                  expert_rows, expert_rows],
        out_specs=[pl.BlockSpec((tq, gw), lambda b, g, i: (b * nq + i, g)),
                   pl.BlockSpec((slab, d), lambda b, g, i: (step(b, g, i), 0)),
                   pl.BlockSpec((d, slab), lambda b, g, i: (0, step(b, g, i)))],
        out_shape=[jax.ShapeDtypeStruct((t, A_Q_HEADS * HEAD_DIM), BF16),
                   jax.ShapeDtypeStruct((ne, d), BF16), jax.ShapeDtypeStruct((d, ne), BF16)],
        scratch_shapes=[pltpu.VMEM((seq, HEAD_DIM), BF16), pltpu.VMEM((seq, 2 * HEAD_DIM), BF16)],
        compiler_params=_cparams("parallel", "parallel", "arbitrary"),
        name="mixer_a_axial_gqa",
    )(qkv, qkv, qkv, ctab, stab, ctab, stab, q_gain.reshape(1, HEAD_DIM), k_gain.reshape(1, HEAD_DIM),
      peer_u, peer_v)


def _window(q0, tq, half, seq):
    span = min(tq + 2 * half, seq)
    start = jnp.clip(q0 - half, 0, seq - span)
    return pl.multiple_of(start, 64), span


def _attn_b_kernel(q_ref, k_ref, v_ref, o_ref):
    tq, seq = q_ref.shape[0], k_ref.shape[0]
    q0 = pl.program_id(1) * tq
    scale = HEAD_DIM ** -0.5
    outs, lses = [], []
    for g, (win, r) in enumerate(B_GROUPS):
        half = win // 2
        start, span = _window(q0, tq, half, seq)
        qpos = q0 + lax.broadcasted_iota(jnp.int32, (tq, span), 0)
        kpos = start + lax.broadcasted_iota(jnp.int32, (tq, span), 1)
        dd = qpos - kpos
        ad = jnp.abs(dd)
        valid = (ad <= half) & ((dd & (r - 1)) == 0)
        adf = ad.astype(F32)
        for hh in range(B_HEADS_PER_GROUP):
            head = g * B_HEADS_PER_GROUP + hh
            cols = slice(head * HEAD_DIM, (head + 1) * HEAD_DIM)
            slope = float(2.0 ** (-8.0 * (head + 1) / B_HEADS))
            s = lax.dot_general(q_ref[:, cols], k_ref[pl.ds(start, span), cols],
                                (((1,), (1,)), ((), ())), preferred_element_type=F32) * scale
            s = jnp.where(valid, s - slope * adf, NEG_INF)
            m = jnp.max(s, axis=-1, keepdims=True)
            p = jnp.exp(s - m)
            l = jnp.sum(p, axis=-1, keepdims=True)
            o = jnp.dot(p.astype(BF16), v_ref[pl.ds(start, span), cols], preferred_element_type=F32)
            outs.append(o / l)
            lses.append(m + jnp.log(l))
    ng = len(B_GROUPS)
    for hh in range(B_HEADS_PER_GROUP):
        hl = [lses[g * B_HEADS_PER_GROUP + hh] for g in range(ng)]
        mx = functools.reduce(jnp.maximum, hl)
        ex = [jnp.exp(x - mx) for x in hl]
        tot = functools.reduce(lambda a, b: a + b, ex)
        for g in range(ng):
            head = g * B_HEADS_PER_GROUP + hh
            o_ref[:, head * HEAD_DIM:(head + 1) * HEAD_DIM] = (outs[head] * (ex[g] / tot)).astype(BF16)


def _mixer_b(qkv, nb, seq, tq=256):
    t = qkv.shape[0]
    nq = seq // tq
    bw = B_HEADS * HEAD_DIM
    return pl.pallas_call(
        _attn_b_kernel,
        grid=(nb, nq),
        in_specs=[pl.BlockSpec((tq, bw), lambda b, i: (b * nq + i, 0)),
                  pl.BlockSpec((seq, bw), lambda b, i: (b, 1)),
                  pl.BlockSpec((seq, bw), lambda b, i: (b, 2))],
        out_specs=pl.BlockSpec((tq, bw), lambda b, i: (b * nq + i, 0)),
        out_shape=jax.ShapeDtypeStruct((t, bw), BF16),
        compiler_params=_cparams("parallel", "arbitrary"),
        name="mixer_b_dilated",
    )(qkv, qkv, qkv)


def _attn_c_kernel(hp_ref, q_ref, k_ref, v_ref, o_ref):
    tq, seq = q_ref.shape[0], k_ref.shape[0]
    grp = q_ref.shape[1] // HEAD_DIM
    g = pl.program_id(1)
    q0 = pl.program_id(2) * tq
    start, span = _window(q0, tq, C_HALF_WINDOW, seq)
    qpos = q0 + lax.broadcasted_iota(jnp.int32, (tq, span), 0)
    kpos = start + lax.broadcasted_iota(jnp.int32, (tq, span), 1)
    ad = jnp.abs(qpos - kpos)
    valid = ad <= C_HALF_WINDOW
    adf = ad.astype(F32)
    kk = k_ref[pl.ds(start, span), :]
    vv = v_ref[pl.ds(start, span), :]
    scale = HEAD_DIM ** -0.5
    for hh in range(grp):
        cols = slice(hh * HEAD_DIM, (hh + 1) * HEAD_DIM)
        slope, sink = hp_ref[g, hh], hp_ref[g, grp + hh]
        s = lax.dot_general(q_ref[:, cols], kk, (((1,), (1,)), ((), ())),
                            preferred_element_type=F32) * scale
        s = jnp.where(valid, s - slope * adf, NEG_INF)
        m = jnp.maximum(jnp.max(s, axis=-1, keepdims=True), sink)
        p = jnp.exp(s - m)
        l = jnp.sum(p, axis=-1, keepdims=True) + jnp.exp(sink - m)
        o = jnp.dot(p.astype(BF16), vv, preferred_element_type=F32)
        o_ref[:, cols] = (o / l).astype(BF16)


def _mixer_c(qkv, sink, nb, seq, tq=256):
    t = qkv.shape[0]
    nq = seq // tq
    grp = C_Q_HEADS // C_KV_HEADS
    gw = grp * HEAD_DIM
    slopes = (2.0 ** (-8.0 * jnp.arange(1, C_Q_HEADS + 1, dtype=F32) / C_Q_HEADS)).reshape(C_KV_HEADS, grp)
    head_params = jnp.concatenate([slopes, sink.astype(F32)], axis=1)
    return pl.pallas_call(
        _attn_c_kernel,
        grid=(nb, C_KV_HEADS, nq),
        in_specs=[pl.BlockSpec(memory_space=pltpu.SMEM),
                  pl.BlockSpec((tq, gw), lambda b, g, i: (b * nq + i, g)),
                  pl.BlockSpec((seq, HEAD_DIM), lambda b, g, i: (b, C_Q_HEADS + g)),
                  pl.BlockSpec((seq, HEAD_DIM), lambda b, g, i: (b, C_Q_HEADS + C_KV_HEADS + g))],
        out_specs=pl.BlockSpec((tq, gw), lambda b, g, i: (b * nq + i, g)),
        out_shape=jax.ShapeDtypeStruct((t, C_Q_HEADS * HEAD_DIM), BF16),
        compiler_params=_cparams("parallel", "parallel", "arbitrary"),
        name="mixer_c_window_sink",
    )(head_params, qkv, qkv, qkv)


def _layer_norm(z, g, b):
    mu = jnp.mean(z, axis=-1, keepdims=True)
    zc = z - mu
    var = jnp.mean(zc * zc, axis=-1, keepdims=True)
    return zc * lax.rsqrt(var + LN_EPS) * g + b


def _branch_kernel(oa_ref, ob_ref, oc_ref, gl_ref, wa_ref, wb_ref, wc_ref, m_ref):
    d = m_ref.shape[1]
    merged = None
    for i, (o_ref, w_ref) in enumerate(((oa_ref, wa_ref), (ob_ref, wb_ref), (oc_ref, wc_ref))):
        y = jnp.dot(o_ref[...], w_ref[...], preferred_element_type=F32)
        term = jax.nn.sigmoid(gl_ref[:, i * d:(i + 1) * d].astype(F32)) * y
        merged = term if merged is None else merged + term
    m_ref[...] = merged.astype(BF16)


def _outproj_kernel(m_ref, x_ref, ga_ref, scf_ref, shf_ref, lg_ref, lb_ref, wo_ref, x1_ref, ht_ref, *, alpha):
    y = jnp.dot(m_ref[...], wo_ref[...], preferred_element_type=F32)
    x1 = _layer_norm(alpha * x_ref[...] + (1.0 + ga_ref[0]) * y, lg_ref[...], lb_ref[...])
    x1_ref[...] = x1
    h2 = x1 * (1.0 + scf_ref[0]) + shf_ref[0]
    ht_ref[...] = h2.T.astype(BF16)


def _merge_project(oa, ob, oc, gl, x2d, mod3, ln_g, ln_b, wa, wb, wc, wo, seq, alpha, tm=512):
    t, d = x2d.shape
    per_b = seq // tm
    row = lambda w: pl.BlockSpec((tm, w), lambda i: (i, 0))
    whole = lambda a: pl.BlockSpec(a.shape, lambda i: (0, 0), pipeline_mode=pl.Buffered(1))
    modspec = lambda k: pl.BlockSpec((1, 1, d), lambda i: (i // per_b, 0, k))
    vec = pl.BlockSpec((1, d), lambda i: (0, 0))
    merged = pl.pallas_call(
        _branch_kernel,
        grid=(t // tm,),
        in_specs=[row(oa.shape[1]), row(ob.shape[1]), row(oc.shape[1]), row(gl.shape[1]),
                  whole(wa), whole(wb), whole(wc)],
        out_specs=row(d),
        out_shape=jax.ShapeDtypeStruct((t, d), BF16),
        compiler_params=_cparams("parallel"),
        name="branch_merge",
    )(oa, ob, oc, gl, wa, wb, wc)
    return pl.pallas_call(
        functools.partial(_outproj_kernel, alpha=alpha),
        grid=(t // tm,),
        in_specs=[row(d), row(d), modspec(2), modspec(4), modspec(3), vec, vec, whole(wo)],
        out_specs=[pl.BlockSpec((tm, d), lambda i: (i, 0)), pl.BlockSpec((d, tm), lambda i: (0, i))],
        out_shape=[jax.ShapeDtypeStruct((t, d), F32), jax.ShapeDtypeStruct((d, t), BF16)],
        compiler_params=_cparams("parallel"),
        name="outproj_ln",
    )(merged, x2d, mod3, mod3, mod3, ln_g.reshape(1, d), ln_b.reshape(1, d), wo)


CAND_ROWS = 4


def _top_exact(s, n, order):
    nrow = lax.broadcasted_iota(jnp.int32, (n, s.shape[1]), 0)

    def step(i, carry):
        s, rank, vals = carry
        m = jnp.max(s, axis=0, keepdims=True)
        first = jnp.min(jnp.where(s == m, order, jnp.iinfo(jnp.int32).max), axis=0, keepdims=True)
        hit = order == first
        return jnp.where(hit, -jnp.inf, s), jnp.where(hit, i, rank), jnp.where(nrow == i, m, vals)

    _, rank, vals = lax.fori_loop(0, n, step, (s, jnp.full(s.shape, n, jnp.int32), jnp.zeros(nrow.shape, F32)))
    return vals, rank


def _candidates(v1, v2):
    k, c = PEER_TOPK, CAND_ROWS
    ki = lax.broadcasted_iota(jnp.int32, v1.shape, 0)
    v1_tail = jnp.where(ki >= c, v1, -jnp.inf)
    cand = jnp.concatenate([v1[i:i + 1] + v2 for i in range(c)] + [v2[j:j + 1] + v1_tail for j in range(c)], axis=0)
    flat = jnp.concatenate([ki + i * k for i in range(c)] +
                           [jnp.where(ki >= c, ki * k + j, -1) for j in range(c)], axis=0)
    return cand, flat


def _route_outputs(s1, s2, v1, v2, rank2, sel, is_ith):
    k, c = PEER_TOPK, CAND_ROWS
    e1 = jnp.exp(v1 - v1[0:1])
    e2 = jnp.exp(v2 - v2[0:1])
    prod = jnp.concatenate([e1[i:i + 1] * e2 for i in range(c)] + [e2[j:j + 1] * e1 for j in range(c)], axis=0)
    z = jnp.sum(jnp.where(sel, prod, 0.0), axis=0, keepdims=True)
    self32 = sel.astype(F32)
    tail = functools.reduce(lambda a, b: a + b, [self32[(c + j) * k:(c + j + 1) * k] for j in range(c)])
    ja = jnp.zeros(s1.shape, F32)
    for i in range(k):
        cnt = jnp.sum(self32[i * k:(i + 1) * k], axis=0, keepdims=True) if i < c else tail[i:i + 1]
        ja = jnp.where(is_ith(i), cnt, ja)
    return (rank2.astype(F32).astype(BF16), jnp.exp(s2 - v2[0:1]).astype(BF16), ja,
            jnp.exp(s1 - v1[0:1]) / z)


def _route_exact(s1, s2):
    k = PEER_TOPK
    rows = lax.broadcasted_iota(jnp.int32, s1.shape, 0)
    v1, rank1 = _top_exact(s1, k, rows)
    v2, rank2 = _top_exact(s2, k, rows)
    cand, flat = _candidates(v1, v2)
    _, crank = _top_exact(cand, k, flat)
    return _route_outputs(s1, s2, v1, v2, rank2, crank < k, lambda i: rank1 == i)


def _sort_network(n):
    def merge(lo, hi, r):
        step = r * 2
        if step < hi - lo:
            yield from merge(lo, hi, step)
            yield from merge(lo + r, hi, step)
            yield from [(i, i + r) for i in range(lo + r, hi - r, step)]
        else:
            yield (lo, lo + r)

    def sort(lo, hi):
        if hi - lo >= 1:
            mid = lo + (hi - lo) // 2
            yield from sort(lo, mid)
            yield from sort(mid + 1, hi)
            yield from merge(lo, hi, 1)

    return tuple(sort(0, n - 1))


def _rep_max(x):
    return jnp.broadcast_to(jnp.max(x, axis=0, keepdims=True), x.shape)


def _rep_sum(x):
    return jnp.broadcast_to(jnp.sum(x, axis=0, keepdims=True), x.shape)


def _top_sorted(groups, n):
    work = [list(g) for g in groups]
    for i, j in _sort_network(len(work[0])):
        for v in work:
            v[i], v[j] = jnp.maximum(v[i], v[j]), jnp.minimum(v[i], v[j])
    vals = [[] for _ in work]
    for it in range(n):
        for v, out in zip(work, vals):
            m = _rep_max(v[0])
            out.append(m)
            hit = v[0] == m
            for k in range(n - 1 - it):
                v[k] = jnp.where(hit, v[k + 1], v[k])
    return vals


def _route_sorted(s1, s2):
    k, c, sub = PEER_TOPK, CAND_ROWS, SUBLANES
    nblk = s1.shape[0] // sub
    b1 = [s1[sub * q:sub * (q + 1)] for q in range(nblk)]
    b2 = [s2[sub * q:sub * (q + 1)] for q in range(nblk)]
    v1, v2 = _top_sorted([b1, b2], k)
    srow = lax.broadcasted_iota(jnp.int32, b1[0].shape, 0)

    def stack(vals):
        return [functools.reduce(lambda acc, t: jnp.where(srow == t, vals[blk * sub + t], acc),
                                 range(1, sub), vals[blk * sub]) for blk in range(k // sub)]

    st1, st2 = stack(v1), stack(v2)
    st1_tail = [jnp.where(srow >= c, st1[0], -jnp.inf)] + st1[1:]
    cand = ([v1[i] + st2[b] for i in range(c) for b in range(k // sub)] +
            [v2[j] + st1_tail[b] for j in range(c) for b in range(k // sub)])
    cv = _top_sorted([cand], k)[0]
    sel = [x >= cv[k - 1] for x in cand]
    e1s = [jnp.exp(x - v1[0]) for x in st1]
    e2s = [jnp.exp(x - v2[0]) for x in st2]
    prod = ([jnp.exp(v1[i] - v1[0]) * e2s[b] for i in range(c) for b in range(k // sub)] +
            [jnp.exp(v2[j] - v2[0]) * e1s[b] for j in range(c) for b in range(k // sub)])
    z = _rep_sum(functools.reduce(lambda a, b: a + b, [jnp.where(s, p, 0.0) for s, p in zip(sel, prod)]))
    ones = [jnp.where(s, 1.0, 0.0) for s in sel]
    per = k // sub
    tail = [functools.reduce(lambda a, b: a + b, [ones[(c + j) * per + b] for j in range(c)]) for b in range(per)]
    counts = []
    for i in range(k):
        if i < c:
            counts.append(_rep_sum(functools.reduce(lambda a, b: a + b, ones[i * per:(i + 1) * per])))
        else:
            counts.append(_rep_sum(jnp.where(srow == i % sub, tail[i // sub], 0.0)))
    inv_z = 1.0 / z
    ja, cf, r2, e2 = [], [], [], []
    for q in range(nblk):
        jq = jnp.zeros(b1[q].shape, F32)
        rq = jnp.zeros(b2[q].shape, F32)
        for i in range(k):
            jq = jnp.where(b1[q] == v1[i], counts[i], jq)
            rq = jnp.where(b2[q] < v2[i], float(i + 1), rq)
        ja.append(jq)
        r2.append(rq)
        cf.append(jnp.exp(b1[q] - v1[0]) * inv_z)
        e2.append(jnp.exp(b2[q] - v2[0]))

    def picked(blocks, thr):
        return _rep_sum(functools.reduce(lambda a, b: a + b, [jnp.where(x >= thr, 1.0, 0.0) for x in blocks]))

    def repeats(vals):
        return functools.reduce(lambda a, b: a | b, [vals[i] == vals[i + 1] for i in range(k - 1)])

    tie = ((picked(b1, v1[k - 1]) != k) | (picked(b2, v2[k - 1]) != k) | (picked(cand, cv[k - 1]) != k) |
           repeats(v1) | repeats(v2) | repeats(cv))
    cat = lambda blocks: jnp.concatenate(blocks, axis=0)
    return (cat(r2).astype(BF16), cat(e2).astype(BF16), cat(ja), cat(cf)), tie.astype(jnp.int32)


def _route_kernel(ht_ref, wqt_ref, keys_ref, r2_ref, e2_ref, ja_ref, cf_ref, qt_scr):
    nk = PEER_N_KEYS
    tm = ht_ref.shape[1]
    qt_scr[...] = jnp.dot(wqt_ref[...], ht_ref[...], preferred_element_type=F32).astype(BF16)

    def head(h, carry):
        q0 = pl.multiple_of(h * 2 * nk, 2 * nk)
        s1 = jnp.dot(keys_ref[0], qt_scr[pl.ds(q0, nk), :], preferred_element_type=F32)
        s2 = jnp.dot(keys_ref[1], qt_scr[pl.ds(q0 + nk, nk), :], preferred_element_type=F32)
        def store(cols, res):
            r2_ref[h, :, cols], e2_ref[h, :, cols], ja_ref[h, :, cols], cf_ref[h, :, cols] = res

        for g in range(tm // LANES):
            cols = slice(g * LANES, (g + 1) * LANES)
            res, tie = _route_sorted(s1[:, cols], s2[:, cols])
            store(cols, res)

            @pl.when(jnp.max(tie) > 0)
            def _(cols=cols):
                store(cols, _route_exact(s1[:, cols], s2[:, cols]))
        return carry

    lax.fori_loop(0, PEER_HEADS, head, 0)


def _peer_route(ht, wqt, keys_bf, tm=512):
    d, t = ht.shape
    shp = lambda dt: jax.ShapeDtypeStruct((PEER_HEADS, PEER_N_KEYS, t), dt)
    spec = pl.BlockSpec((PEER_HEADS, PEER_N_KEYS, tm), lambda i: (0, 0, i))
    return pl.pallas_call(
        _route_kernel,
        grid=(t // tm,),
        in_specs=[pl.BlockSpec((d, tm), lambda i: (0, i)),
                  pl.BlockSpec(wqt.shape, lambda i: (0, 0)),
                  pl.BlockSpec(keys_bf.shape, lambda i: (0, 0, 0))],
        out_specs=[spec] * 4,
        out_shape=[shp(BF16), shp(BF16), shp(F32), shp(F32)],
        scratch_shapes=[pltpu.VMEM((wqt.shape[0], tm), BF16)],
        compiler_params=_cparams("parallel"),
        name="peer_route",
    )(ht, wqt, keys_bf)


def _cast_kernel(x_ref, o_ref, *, transpose):
    x = x_ref[...]
    o_ref[...] = (x.T if transpose else x).astype(BF16)


def _layer_bf16(w_all, layer, transpose, tr=512):
    _, r, c = w_all.shape
    return pl.pallas_call(
        functools.partial(_cast_kernel, transpose=transpose),
        grid=(r // tr,),
        in_specs=[pl.BlockSpec((None, tr, c), lambda i: (layer, i, 0))],
        out_specs=pl.BlockSpec((c, tr), lambda i: (0, i)) if transpose else pl.BlockSpec((tr, c), lambda i: (i, 0)),
        out_shape=jax.ShapeDtypeStruct((c, r) if transpose else (r, c), BF16),
        compiler_params=_cparams("parallel"),
        name="transpose_cast" if transpose else "cast",
    )(w_all)


EXP_CHUNK = 512
GATE_LANES = 256

def _gated_hidden(gel, a0, r2_ref, e2_ref, ja_ref, cf_ref, w_ref):
    nk = PEER_N_KEYS
    na = gel.shape[0] // nk
    zero = jnp.zeros((), BF16)
    tm = gel.shape[1]

    def row(ref, h, aa):
        return ref[h, pl.ds(a0 + aa, 1), :].astype(BF16)

    ja_rows = [[row(ja_ref, h, aa) for aa in range(na)] for h in range(PEER_HEADS)]
    cf_rows = [[row(cf_ref, h, aa) for aa in range(na)] for h in range(PEER_HEADS)]
    for lg in range(tm // GATE_LANES):
        lanes = slice(lg * GATE_LANES, (lg + 1) * GATE_LANES)
        acc = [None] * na
        for h in range(PEER_HEADS):
            r2c, e2c = r2_ref[h, :, lanes], e2_ref[h, :, lanes]
            for aa in range(na):
                term = jnp.where(r2c < ja_rows[h][aa][:, lanes], e2c * cf_rows[h][aa][:, lanes], zero)
                acc[aa] = term if acc[aa] is None else acc[aa] + term
        for aa in range(na):
            rows = slice(aa * nk, (aa + 1) * nk)
            w_ref[rows, lanes] = acc[aa] * gel[rows, lanes]


def _experts_kernel(ht_ref, u_ref, vt_ref, r2_ref, e2_ref, ja_ref, cf_ref, yt_ref, w_scr):
    j = pl.program_id(1)
    te = u_ref.shape[0]

    @pl.when(j == 0)
    def _():
        yt_ref[...] = jnp.zeros_like(yt_ref)

    ht = ht_ref[...]
    nchunk = te // EXP_CHUNK

    def hidden(c):
        return jnp.dot(u_ref[c * EXP_CHUNK:(c + 1) * EXP_CHUNK, :], ht, preferred_element_type=F32)

    act = hidden(0)
    for c in range(nchunk):
        nxt = hidden(c + 1) if c + 1 < nchunk else None
        gel = (0.5 * act * (1.0 + lax.erf(act * (2.0 ** -0.5)))).astype(BF16)
        a0 = j * (te // PEER_N_KEYS) + c * (EXP_CHUNK // PEER_N_KEYS)
        _gated_hidden(gel, a0, r2_ref, e2_ref, ja_ref, cf_ref, w_scr.at[c])
        yt_ref[...] += jnp.dot(vt_ref[:, c * EXP_CHUNK:(c + 1) * EXP_CHUNK], w_scr[c],
                               preferred_element_type=F32)
        act = nxt


def _peer_experts(ht, u_bf, vt_bf, r2, e2, ja, cf, tm=512, te=1024):
    d, t = ht.shape
    ne = u_bf.shape[0]
    side = pl.BlockSpec((PEER_HEADS, PEER_N_KEYS, tm), lambda i, j: (0, 0, i))
    return pl.pallas_call(
        _experts_kernel,
        grid=(t // tm, ne // te),
        in_specs=[pl.BlockSpec((d, tm), lambda i, j: (0, i)),
                  pl.BlockSpec((te, d), lambda i, j: (j, 0)),
                  pl.BlockSpec((d, te), lambda i, j: (0, j)),
                  side, side, side, side],
        out_specs=pl.BlockSpec((d, tm), lambda i, j: (0, i)),
        out_shape=jax.ShapeDtypeStruct((d, t), F32),
        scratch_shapes=[pltpu.VMEM((te // EXP_CHUNK, EXP_CHUNK, tm), BF16)],
        compiler_params=_cparams("parallel", "arbitrary"),
        name="peer_experts",
    )(ht, u_bf, vt_bf, r2, e2, ja, cf)


def _resid_ln_kernel(x_ref, yt_ref, gf_ref, lg_ref, lb_ref, o_ref, *, alpha):
    o_ref[...] = _layer_norm(alpha * x_ref[...] + (1.0 + gf_ref[0]) * yt_ref[...].T, lg_ref[...], lb_ref[...])


def _resid_ln(x2d, yt, mod3, ln_g, ln_b, seq, alpha, tm=512):
    t, d = x2d.shape
    per_b = seq // tm
    row = pl.BlockSpec((tm, d), lambda i: (i, 0))
    vec = pl.BlockSpec((1, d), lambda i: (0, 0))
    return pl.pallas_call(
        functools.partial(_resid_ln_kernel, alpha=alpha),
        grid=(t // tm,),
        in_specs=[row, pl.BlockSpec((d, tm), lambda i: (0, i)),
                  pl.BlockSpec((1, 1, d), lambda i: (i // per_b, 0, 5)), vec, vec],
        out_specs=row,
        out_shape=jax.ShapeDtypeStruct((t, d), F32),
        compiler_params=_cparams("parallel"),
        name="resid_ln",
    )(x2d, yt, mod3, ln_g.reshape(1, d), ln_b.reshape(1, d))


def kernel(x, c, w_mod, b_mod, w_in, a_q_gain, a_k_gain, c_sink, w_pa, w_pb, w_pc, w_o,
           ln1_g, ln1_b, peer_wq, peer_keys, peer_u, peer_v, ln2_g, ln2_b):
    nb, seq, d = x.shape
    depth = w_mod.shape[0]
    alpha = float((2 * depth) ** 0.25)
    ctab, stab = _rope_tables(seq)
    mod = _modulation(c, w_mod, b_mod)
    x2d = x.reshape(nb * seq, d)
    for l in range(depth):
        mod3 = mod[l].reshape(nb, 1, 6 * d)
        qkv_a, qkv_b, qkv_c, gl = _in_projection(x2d, mod3, w_in, l, seq)
        oa, u_bf, vt_bf = _mixer_a(qkv_a, ctab, stab, a_q_gain[l], a_k_gain[l], peer_u, peer_v, l, nb, seq)
        ob = _mixer_b(qkv_b, nb, seq)
        oc = _mixer_c(qkv_c, c_sink[l], nb, seq)
        x1, ht = _merge_project(oa, ob, oc, gl, x2d, mod3, ln1_g[l], ln1_b[l],
                                w_pa[l].astype(BF16), w_pb[l].astype(BF16), w_pc[l].astype(BF16),
                                w_o[l].astype(BF16), seq, alpha)
        r2, e2, ja, cf = _peer_route(ht, _layer_bf16(peer_wq, l, True), peer_keys[l].astype(BF16))
        yt = _peer_experts(ht, u_bf, vt_bf, r2, e2, ja, cf)
        x2d = _resid_ln(x1, yt, mod3, ln2_g[l], ln2_b[l], seq, alpha)
    return x2d.reshape(nb, seq, d)
```

```python
import functools

import numpy as np
import jax
import jax.numpy as jnp
from jax import lax
from jax.experimental import pallas as pl
from jax.experimental.pallas import tpu as pltpu

F32 = jnp.float32
BF16 = jnp.bfloat16

HEAD_DIM = 128
A_Q_HEADS, A_KV_HEADS = 8, 2
B_GROUPS = ((128, 1), (512, 4), (2048, 16))
B_HEADS_PER_GROUP = 2
B_HEADS = B_HEADS_PER_GROUP * len(B_GROUPS)
C_Q_HEADS, C_KV_HEADS = 8, 2
C_HALF_WINDOW = 128
N_BRANCHES = 3
GRID_W = 64
ROPE_THETA = 10000.0
PEER_HEADS = 8
PEER_N_KEYS = 128
PEER_TOPK = 16
LN_EPS = 1e-5
QK_EPS = 1e-6
NEG_INF = -1e30

LANES = 128
SUBLANES = 8
BF16_ROWS = 16
VMEM_LIMIT = 56 * 1024 * 1024
MERGE_VMEM_LIMIT = 58 * 1024 * 1024


def _cparams(*sem, vmem=VMEM_LIMIT):
    return pltpu.CompilerParams(dimension_semantics=sem, vmem_limit_bytes=vmem)


def _mod_kernel(cb_ref, w_ref, b_ref, o_ref):
    nb, tn = cb_ref.shape[0], w_ref.shape[2]
    rows = []
    for b in range(nb):
        cb = cb_ref[b]
        pieces = [jnp.sum(w_ref[0, :, n * LANES:(n + 1) * LANES] * cb, axis=0, keepdims=True)
                  for n in range(tn // LANES)]
        rows.append(jnp.concatenate(pieces, axis=1))
    o_ref[0] = jnp.concatenate(rows, axis=0) + b_ref[0]


def _modulation(c, w_mod, b_mod, tn=1024):
    nl, d, n6 = w_mod.shape
    nb = c.shape[0]
    cb = jnp.broadcast_to(c[:, :, None], (nb, d, LANES))
    return pl.pallas_call(
        _mod_kernel,
        grid=(nl, n6 // tn),
        in_specs=[pl.BlockSpec((nb, d, LANES), lambda l, j: (0, 0, 0)),
                  pl.BlockSpec((1, d, tn), lambda l, j: (l, 0, j)),
                  pl.BlockSpec((1, 1, tn), lambda l, j: (l, 0, j))],
        out_specs=pl.BlockSpec((1, nb, tn), lambda l, j: (l, 0, j)),
        out_shape=jax.ShapeDtypeStruct((nl, nb, n6), F32),
        compiler_params=_cparams("parallel", "parallel"),
        name="adaln_modulation",
    )(cb, w_mod, b_mod.reshape(nl, 1, n6))


IN_TN = 768
IN_TILES = (2, 3, 2, 8)


def _inproj_kernel(x_ref, sc_ref, sh_ref, w_ref, oa_ref, ob_ref, oc_ref, og_ref, h_scr):
    j = pl.program_id(1)

    @pl.when(j == 0)
    def _():
        h_scr[...] = (x_ref[...] * (1.0 + sc_ref[0]) + sh_ref[0]).astype(BF16)

    lo = 0
    for ref, n in zip((oa_ref, ob_ref, oc_ref, og_ref), IN_TILES):
        @pl.when((j >= lo) & (j < lo + n))
        def _(ref=ref):
            ref[...] = jnp.dot(h_scr[...], w_ref[...].astype(BF16), preferred_element_type=F32).astype(BF16)
        lo += n


def _in_projection(x2d, mod3, w_all, layer, seq, tm=1024):
    t, d = x2d.shape
    tm = min(tm, seq)
    per_b = seq // tm
    starts = np.cumsum((0,) + IN_TILES[:-1]).tolist()

    def out_spec(lo, n):
        return pl.BlockSpec((tm, IN_TN), lambda i, j: (i, jnp.clip(j - lo, 0, n - 1)))

    return pl.pallas_call(
        _inproj_kernel,
        grid=(t // tm, sum(IN_TILES)),
        in_specs=[pl.BlockSpec((tm, d), lambda i, j: (i, 0)),
                  pl.BlockSpec((1, 1, d), lambda i, j: (i // per_b, 0, 1)),
                  pl.BlockSpec((1, 1, d), lambda i, j: (i // per_b, 0, 0)),
                  pl.BlockSpec((None, d, IN_TN), lambda i, j: (layer, 0, j))],
        out_specs=[out_spec(lo, n) for lo, n in zip(starts, IN_TILES)],
        out_shape=[jax.ShapeDtypeStruct((t, n * IN_TN), BF16) for n in IN_TILES],
        scratch_shapes=[pltpu.VMEM((tm, d), BF16)],
        compiler_params=_cparams("parallel", "arbitrary"),
        name="in_projection",
    )(x2d, mod3, mod3, w_all)


def _rope_tables(seq):
    rows = seq // GRID_W
    row = jnp.repeat(jnp.arange(rows, dtype=F32), GRID_W)
    col = jnp.tile(jnp.arange(GRID_W, dtype=F32), rows)
    quarter = HEAD_DIM // 4
    inv = ROPE_THETA ** (-jnp.arange(quarter, dtype=F32) / quarter)
    ar, ac = row[:, None] * inv, col[:, None] * inv
    ctab = jnp.concatenate([jnp.cos(ar), jnp.cos(ar), jnp.cos(ac), jnp.cos(ac)], axis=1)
    stab = jnp.concatenate([-jnp.sin(ar), jnp.sin(ar), -jnp.sin(ac), jnp.sin(ac)], axis=1)
    return ctab, stab


def _norm_rope(xf, gain, ctab, stab):
    hd = xf.shape[1]
    r = lax.broadcasted_iota(jnp.int32, (hd, hd), 0)
    c = lax.broadcasted_iota(jnp.int32, (hd, hd), 1)
    mean_mat = jnp.full((hd, hd), 1.0 / hd, BF16)
    swap_mat = jnp.where(r == (c ^ (hd // 4)), 1.0, 0.0).astype(BF16)
    ms = jnp.dot((xf * xf).astype(BF16), mean_mat, preferred_element_type=F32)
    xn = xf * lax.rsqrt(ms + QK_EPS) * gain
    partner = jnp.dot(xn.astype(BF16), swap_mat, preferred_element_type=F32)
    return xn * ctab + partner * stab


def _attn_a_kernel(q_ref, k_ref, v_ref, cq_ref, sq_ref, ck_ref, sk_ref, qg_ref, kg_ref, eu_ref, ev_ref,
                   o_ref, ub_ref, vt_ref, kp_scr, vp_scr, *, kc):
    tq, seq = q_ref.shape[0], k_ref.shape[0]
    grp = q_ref.shape[1] // HEAD_DIM
    ub_ref[...] = eu_ref[...].astype(BF16)
    vt_ref[...] = ev_ref[...].T.astype(BF16)

    @pl.when(pl.program_id(2) == 0)
    def _():
        kp_scr[...] = _norm_rope(k_ref[...].astype(F32), kg_ref[...], ck_ref[...], sk_ref[...]).astype(BF16)
        vp_scr[:, :HEAD_DIM] = v_ref[...]
        vp_scr[:, HEAD_DIM:] = jnp.ones((seq, HEAD_DIM), BF16)

    cq, sq, qg = cq_ref[...], sq_ref[...], qg_ref[...]
    scale = HEAD_DIM ** -0.5
    q4 = jnp.concatenate(
        [(_norm_rope(q_ref[:, h * HEAD_DIM:(h + 1) * HEAD_DIM].astype(F32), qg, cq, sq) * scale).astype(BF16)
         for h in range(grp)], axis=0)

    def scores(c):
        return lax.dot_general(q4, kp_scr[c * kc:(c + 1) * kc, :], (((1,), (1,)), ((), ())),
                               preferred_element_type=F32)

    nchunk = seq // kc
    s = scores(0)
    m = acc = None
    for c in range(nchunk):
        nxt = scores(c + 1) if c + 1 < nchunk else None
        smax = jnp.max(s, axis=-1, keepdims=True)
        mn = smax if m is None else jnp.maximum(m, smax)
        p = jnp.exp(s - mn).astype(BF16)
        pv = jnp.dot(p, vp_scr[c * kc:(c + 1) * kc, :], preferred_element_type=F32)
        acc = pv if m is None else jnp.exp(m - mn) * acc + pv
        m, s = mn, nxt
    o = acc[:, :HEAD_DIM] / acc[:, HEAD_DIM:HEAD_DIM + 1]
    for h in range(grp):
        o_ref[:, h * HEAD_DIM:(h + 1) * HEAD_DIM] = o[h * tq:(h + 1) * tq].astype(BF16)


def _mixer_a(qkv, ctab, stab, q_gain, k_gain, peer_u, peer_v, layer, nb, seq, tq=256, kc=2048):
    t = qkv.shape[0]
    nq = seq // tq
    kc = min(kc, seq)
    gw = (A_Q_HEADS // A_KV_HEADS) * HEAD_DIM
    k_blk = A_Q_HEADS
    v_blk = A_Q_HEADS + A_KV_HEADS
    _, ne, d = peer_u.shape
    slab = ne // (nb * A_KV_HEADS * nq)
    step = lambda b, g, i: (b * A_KV_HEADS + g) * nq + i
    expert_rows = pl.BlockSpec((None, slab, d), lambda b, g, i: (layer, step(b, g, i), 0))
    return pl.pallas_call(
        functools.partial(_attn_a_kernel, kc=kc),
        grid=(nb, A_KV_HEADS, nq),
        in_specs=[pl.BlockSpec((tq, gw), lambda b, g, i: (b * nq + i, g)),
                  pl.BlockSpec((seq, HEAD_DIM), lambda b, g, i: (b, k_blk + g)),
                  pl.BlockSpec((seq, HEAD_DIM), lambda b, g, i: (b, v_blk + g)),
                  pl.BlockSpec((tq, HEAD_DIM), lambda b, g, i: (i, 0)),
                  pl.BlockSpec((tq, HEAD_DIM), lambda b, g, i: (i, 0)),
                  pl.BlockSpec((seq, HEAD_DIM), lambda b, g, i: (0, 0)),
                  pl.BlockSpec((seq, HEAD_DIM), lambda b, g, i: (0, 0)),
                  pl.BlockSpec((1, HEAD_DIM), lambda b, g, i: (0, 0)),
                  pl.BlockSpec((1, HEAD_DIM), lambda b, g, i: (0, 0)),
                  expert_rows, expert_rows],
        out_specs=[pl.BlockSpec((tq, gw), lambda b, g, i: (b * nq + i, g)),
                   pl.BlockSpec((slab, d), lambda b, g, i: (step(b, g, i), 0)),
                   pl.BlockSpec((d, slab), lambda b, g, i: (0, step(b, g, i)))],
        out_shape=[jax.ShapeDtypeStruct((t, A_Q_HEADS * HEAD_DIM), BF16),
                   jax.ShapeDtypeStruct((ne, d), BF16), jax.ShapeDtypeStruct((d, ne), BF16)],
        scratch_shapes=[pltpu.VMEM((seq, HEAD_DIM), BF16), pltpu.VMEM((seq, 2 * HEAD_DIM), BF16)],
        compiler_params=_cparams("parallel", "parallel", "arbitrary"),
        name="mixer_a_axial_gqa",
    )(qkv, qkv, qkv, ctab, stab, ctab, stab, q_gain.reshape(1, HEAD_DIM), k_gain.reshape(1, HEAD_DIM),
      peer_u, peer_v)


def _window(q0, tq, half, seq):
    span = min(tq + 2 * half, seq)
    start = jnp.clip(q0 - half, 0, seq - span)
    return pl.multiple_of(start, 64), span


def _attn_b_kernel(q_ref, k_ref, v_ref, o_ref):
    tq, seq = q_ref.shape[0], k_ref.shape[0]
    q0 = pl.program_id(1) * tq
    scale = HEAD_DIM ** -0.5
    outs, lses = [], []
    for g, (win, r) in enumerate(B_GROUPS):
        half = win // 2
        start, span = _window(q0, tq, half, seq)
        qpos = q0 + lax.broadcasted_iota(jnp.int32, (tq, span), 0)
        kpos = start + lax.broadcasted_iota(jnp.int32, (tq, span), 1)
        dd = qpos - kpos
        ad = jnp.abs(dd)
        valid = (ad <= half) & ((dd & (r - 1)) == 0)
        adf = ad.astype(F32)
        for hh in range(B_HEADS_PER_GROUP):
            head = g * B_HEADS_PER_GROUP + hh
            cols = slice(head * HEAD_DIM, (head + 1) * HEAD_DIM)
            slope = float(2.0 ** (-8.0 * (head + 1) / B_HEADS))
            s = lax.dot_general(q_ref[:, cols], k_ref[pl.ds(start, span), cols],
                                (((1,), (1,)), ((), ())), preferred_element_type=F32) * scale
            s = jnp.where(valid, s - slope * adf, NEG_INF)
            m = jnp.max(s, axis=-1, keepdims=True)
            p = jnp.exp(s - m)
            l = jnp.sum(p, axis=-1, keepdims=True)
            o = jnp.dot(p.astype(BF16), v_ref[pl.ds(start, span), cols], preferred_element_type=F32)
            outs.append(o / l)
            lses.append(m + jnp.log(l))
    ng = len(B_GROUPS)
    for hh in range(B_HEADS_PER_GROUP):
        hl = [lses[g * B_HEADS_PER_GROUP + hh] for g in range(ng)]
        mx = functools.reduce(jnp.maximum, hl)
        ex = [jnp.exp(x - mx) for x in hl]
        tot = functools.reduce(lambda a, b: a + b, ex)
        for g in range(ng):
            head = g * B_HEADS_PER_GROUP + hh
            o_ref[:, head * HEAD_DIM:(head + 1) * HEAD_DIM] = (outs[head] * (ex[g] / tot)).astype(BF16)


def _mixer_b(qkv, nb, seq, tq=256):
    t = qkv.shape[0]
    nq = seq // tq
    bw = B_HEADS * HEAD_DIM
    return pl.pallas_call(
        _attn_b_kernel,
        grid=(nb, nq),
        in_specs=[pl.BlockSpec((tq, bw), lambda b, i: (b * nq + i, 0)),
                  pl.BlockSpec((seq, bw), lambda b, i: (b, 1)),
                  pl.BlockSpec((seq, bw), lambda b, i: (b, 2))],
        out_specs=pl.BlockSpec((tq, bw), lambda b, i: (b * nq + i, 0)),
        out_shape=jax.ShapeDtypeStruct((t, bw), BF16),
        compiler_params=_cparams("parallel", "arbitrary"),
        name="mixer_b_dilated",
    )(qkv, qkv, qkv)


def _attn_c_kernel(hp_ref, q_ref, k_ref, v_ref, o_ref):
    tq, seq = q_ref.shape[0], k_ref.shape[0]
    grp = q_ref.shape[1] // HEAD_DIM
    g = pl.program_id(1)
    q0 = pl.program_id(2) * tq
    start, span = _window(q0, tq, C_HALF_WINDOW, seq)
    qpos = q0 + lax.broadcasted_iota(jnp.int32, (tq, span), 0)
    kpos = start + lax.broadcasted_iota(jnp.int32, (tq, span), 1)
    ad = jnp.abs(qpos - kpos)
    valid = ad <= C_HALF_WINDOW
    adf = ad.astype(F32)
    kk = k_ref[pl.ds(start, span), :]
    vv = v_ref[pl.ds(start, span), :]
    scale = HEAD_DIM ** -0.5
    for hh in range(grp):
        cols = slice(hh * HEAD_DIM, (hh + 1) * HEAD_DIM)
        slope, sink = hp_ref[g, hh], hp_ref[g, grp + hh]
        s = lax.dot_general(q_ref[:, cols], kk, (((1,), (1,)), ((), ())),
                            preferred_element_type=F32) * scale
        s = jnp.where(valid, s - slope * adf, NEG_INF)
        m = jnp.maximum(jnp.max(s, axis=-1, keepdims=True), sink)
        p = jnp.exp(s - m)
        l = jnp.sum(p, axis=-1, keepdims=True) + jnp.exp(sink - m)
        o = jnp.dot(p.astype(BF16), vv, preferred_element_type=F32)
        o_ref[:, cols] = (o / l).astype(BF16)


def _mixer_c(qkv, sink, nb, seq, tq=256):
    t = qkv.shape[0]
    nq = seq // tq
    grp = C_Q_HEADS // C_KV_HEADS
    gw = grp * HEAD_DIM
    slopes = (2.0 ** (-8.0 * jnp.arange(1, C_Q_HEADS + 1, dtype=F32) / C_Q_HEADS)).reshape(C_KV_HEADS, grp)
    head_params = jnp.concatenate([slopes, sink.astype(F32)], axis=1)
    return pl.pallas_call(
        _attn_c_kernel,
        grid=(nb, C_KV_HEADS, nq),
        in_specs=[pl.BlockSpec(memory_space=pltpu.SMEM),
                  pl.BlockSpec((tq, gw), lambda b, g, i: (b * nq + i, g)),
                  pl.BlockSpec((seq, HEAD_DIM), lambda b, g, i: (b, C_Q_HEADS + g)),
                  pl.BlockSpec((seq, HEAD_DIM), lambda b, g, i: (b, C_Q_HEADS + C_KV_HEADS + g))],
        out_specs=pl.BlockSpec((tq, gw), lambda b, g, i: (b * nq + i, g)),
        out_shape=jax.ShapeDtypeStruct((t, C_Q_HEADS * HEAD_DIM), BF16),
        compiler_params=_cparams("parallel", "parallel", "arbitrary"),
        name="mixer_c_window_sink",
    )(head_params, qkv, qkv, qkv)


def _layer_norm(z, g, b):
    mu = jnp.mean(z, axis=-1, keepdims=True)
    zc = z - mu
    var = jnp.mean(zc * zc, axis=-1, keepdims=True)
    return zc * lax.rsqrt(var + LN_EPS) * g + b


def _branch_kernel(oa_ref, ob_ref, oc_ref, gl_ref, wa_ref, wb_ref, wc_ref, m_ref):
    d = m_ref.shape[1]
    merged = None
    for i, (o_ref, w_ref) in enumerate(((oa_ref, wa_ref), (ob_ref, wb_ref), (oc_ref, wc_ref))):
        y = jnp.dot(o_ref[...], w_ref[...], preferred_element_type=F32)
        term = jax.nn.sigmoid(gl_ref[:, i * d:(i + 1) * d].astype(F32)) * y
        merged = term if merged is None else merged + term
    m_ref[...] = merged.astype(BF16)


def _outproj_kernel(m_ref, x_ref, ga_ref, scf_ref, shf_ref, lg_ref, lb_ref, wo_ref, x1_ref, ht_ref, *, alpha):
    y = jnp.dot(m_ref[...], wo_ref[...], preferred_element_type=F32)
    x1 = _layer_norm(alpha * x_ref[...] + (1.0 + ga_ref[0]) * y, lg_ref[...], lb_ref[...])
    x1_ref[...] = x1
    h2 = x1 * (1.0 + scf_ref[0]) + shf_ref[0]
    ht_ref[...] = h2.T.astype(BF16)


def _merge_project(oa, ob, oc, gl, x2d, mod3, ln_g, ln_b, wa, wb, wc, wo, seq, alpha, tm=512):
    t, d = x2d.shape
    per_b = seq // tm
    row = lambda w: pl.BlockSpec((tm, w), lambda i: (i, 0))
    whole = lambda a: pl.BlockSpec(a.shape, lambda i: (0, 0), pipeline_mode=pl.Buffered(1))
    modspec = lambda k: pl.BlockSpec((1, 1, d), lambda i: (i // per_b, 0, k))
    vec = pl.BlockSpec((1, d), lambda i: (0, 0))
    merged = pl.pallas_call(
        _branch_kernel,
        grid=(t // tm,),
        in_specs=[row(oa.shape[1]), row(ob.shape[1]), row(oc.shape[1]), row(gl.shape[1]),
                  whole(wa), whole(wb), whole(wc)],
        out_specs=row(d),
        out_shape=jax.ShapeDtypeStruct((t, d), BF16),
        compiler_params=_cparams("parallel"),
        name="branch_merge",
    )(oa, ob, oc, gl, wa, wb, wc)
    return pl.pallas_call(
        functools.partial(_outproj_kernel, alpha=alpha),
        grid=(t // tm,),
        in_specs=[row(d), row(d), modspec(2), modspec(4), modspec(3), vec, vec, whole(wo)],
        out_specs=[pl.BlockSpec((tm, d), lambda i: (i, 0)), pl.BlockSpec((d, tm), lambda i: (0, i))],
        out_shape=[jax.ShapeDtypeStruct((t, d), F32), jax.ShapeDtypeStruct((d, t), BF16)],
        compiler_params=_cparams("parallel"),
        name="outproj_ln",
    )(merged, x2d, mod3, mod3, mod3, ln_g.reshape(1, d), ln_b.reshape(1, d), wo)


CAND_ROWS = 4


def _top_exact(s, n, order):
    nrow = lax.broadcasted_iota(jnp.int32, (n, s.shape[1]), 0)

    def step(i, carry):
        s, rank, vals = carry
        m = jnp.max(s, axis=0, keepdims=True)
        first = jnp.min(jnp.where(s == m, order, jnp.iinfo(jnp.int32).max), axis=0, keepdims=True)
        hit = order == first
        return jnp.where(hit, -jnp.inf, s), jnp.where(hit, i, rank), jnp.where(nrow == i, m, vals)

    _, rank, vals = lax.fori_loop(0, n, step, (s, jnp.full(s.shape, n, jnp.int32), jnp.zeros(nrow.shape, F32)))
    return vals, rank


def _candidates(v1, v2):
    k, c = PEER_TOPK, CAND_ROWS
    ki = lax.broadcasted_iota(jnp.int32, v1.shape, 0)
    v1_tail = jnp.where(ki >= c, v1, -jnp.inf)
    cand = jnp.concatenate([v1[i:i + 1] + v2 for i in range(c)] + [v2[j:j + 1] + v1_tail for j in range(c)], axis=0)
    flat = jnp.concatenate([ki + i * k for i in range(c)] +
                           [jnp.where(ki >= c, ki * k + j, -1) for j in range(c)], axis=0)
    return cand, flat


def _route_outputs(s1, s2, v1, v2, rank2, sel, is_ith):
    k, c = PEER_TOPK, CAND_ROWS
    e1 = jnp.exp(v1 - v1[0:1])
    e2 = jnp.exp(v2 - v2[0:1])
    prod = jnp.concatenate([e1[i:i + 1] * e2 for i in range(c)] + [e2[j:j + 1] * e1 for j in range(c)], axis=0)
    z = jnp.sum(jnp.where(sel, prod, 0.0), axis=0, keepdims=True)
    self32 = sel.astype(F32)
    tail = functools.reduce(lambda a, b: a + b, [self32[(c + j) * k:(c + j + 1) * k] for j in range(c)])
    ja = jnp.zeros(s1.shape, F32)
    for i in range(k):
        cnt = jnp.sum(self32[i * k:(i + 1) * k], axis=0, keepdims=True) if i < c else tail[i:i + 1]
        ja = jnp.where(is_ith(i), cnt, ja)
    return (rank2.astype(F32).astype(BF16), jnp.exp(s2 - v2[0:1]).astype(BF16), ja,
            jnp.exp(s1 - v1[0:1]) / z)


def _route_exact(s1, s2):
    k = PEER_TOPK
    rows = lax.broadcasted_iota(jnp.int32, s1.shape, 0)
    v1, rank1 = _top_exact(s1, k, rows)
    v2, rank2 = _top_exact(s2, k, rows)
    cand, flat = _candidates(v1, v2)
    _, crank = _top_exact(cand, k, flat)
    return _route_outputs(s1, s2, v1, v2, rank2, crank < k, lambda i: rank1 == i)


def _sort_network(n):
    def merge(lo, hi, r):
        step = r * 2
        if step < hi - lo:
            yield from merge(lo, hi, step)
            yield from merge(lo + r, hi, step)
            yield from [(i, i + r) for i in range(lo + r, hi - r, step)]
        else:
            yield (lo, lo + r)

    def sort(lo, hi):
        if hi - lo >= 1:
            mid = lo + (hi - lo) // 2
            yield from sort(lo, mid)
            yield from sort(mid + 1, hi)
            yield from merge(lo, hi, 1)

    return tuple(sort(0, n - 1))


def _rep_max(x):
    return jnp.broadcast_to(jnp.max(x, axis=0, keepdims=True), x.shape)


def _rep_sum(x):
    return jnp.broadcast_to(jnp.sum(x, axis=0, keepdims=True), x.shape)


def _top_sorted(groups, n):
    work = [list(g) for g in groups]
    for i, j in _sort_network(len(work[0])):
        for v in work:
            v[i], v[j] = jnp.maximum(v[i], v[j]), jnp.minimum(v[i], v[j])
    vals = [[] for _ in work]
    for it in range(n):
        for v, out in zip(work, vals):
            m = _rep_max(v[0])
            out.append(m)
            hit = v[0] == m
            for k in range(n - 1 - it):
                v[k] = jnp.where(hit, v[k + 1], v[k])
    return vals


def _route_sorted(s1, s2):
    k, c, sub = PEER_TOPK, CAND_ROWS, SUBLANES
    nblk = s1.shape[0] // sub
    b1 = [s1[sub * q:sub * (q + 1)] for q in range(nblk)]
    b2 = [s2[sub * q:sub * (q + 1)] for q in range(nblk)]
    v1, v2 = _top_sorted([b1, b2], k)
    srow = lax.broadcasted_iota(jnp.int32, b1[0].shape, 0)

    def stack(vals):
        return [functools.reduce(lambda acc, t: jnp.where(srow == t, vals[blk * sub + t], acc),
                                 range(1, sub), vals[blk * sub]) for blk in range(k // sub)]

    st1, st2 = stack(v1), stack(v2)
    st1_tail = [jnp.where(srow >= c, st1[0], -jnp.inf)] + st1[1:]
    cand = ([v1[i] + st2[b] for i in range(c) for b in range(k // sub)] +
            [v2[j] + st1_tail[b] for j in range(c) for b in range(k // sub)])
    cv = _top_sorted([cand], k)[0]
    sel = [x >= cv[k - 1] for x in cand]
    e1s = [jnp.exp(x - v1[0]) for x in st1]
    e2s = [jnp.exp(x - v2[0]) for x in st2]
    prod = ([jnp.exp(v1[i] - v1[0]) * e2s[b] for i in range(c) for b in range(k // sub)] +
            [jnp.exp(v2[j] - v2[0]) * e1s[b] for j in range(c) for b in range(k // sub)])
    z = _rep_sum(functools.reduce(lambda a, b: a + b, [jnp.where(s, p, 0.0) for s, p in zip(sel, prod)]))
    ones = [jnp.where(s, 1.0, 0.0) for s in sel]
    per = k // sub
    tail = [functools.reduce(lambda a, b: a + b, [ones[(c + j) * per + b] for j in range(c)]) for b in range(per)]
    counts = []
    for i in range(k):
        if i < c:
            counts.append(_rep_sum(functools.reduce(lambda a, b: a + b, ones[i * per:(i + 1) * per])))
        else:
            counts.append(_rep_sum(jnp.where(srow == i % sub, tail[i // sub], 0.0)))
    inv_z = 1.0 / z
    ja, cf, r2, e2 = [], [], [], []
    for q in range(nblk):
        jq = jnp.zeros(b1[q].shape, F32)
        rq = jnp.zeros(b2[q].shape, F32)
        for i in range(k):
            jq = jnp.where(b1[q] == v1[i], counts[i], jq)
            rq = jnp.where(b2[q] < v2[i], float(i + 1), rq)
        ja.append(jq)
        r2.append(rq)
        cf.append(jnp.exp(b1[q] - v1[0]) * inv_z)
        e2.append(jnp.exp(b2[q] - v2[0]))

    def picked(blocks, thr):
        return _rep_sum(functools.reduce(lambda a, b: a + b, [jnp.where(x >= thr, 1.0, 0.0) for x in blocks]))

    def repeats(vals):
        return functools.reduce(lambda a, b: a | b, [vals[i] == vals[i + 1] for i in range(k - 1)])

    tie = ((picked(b1, v1[k - 1]) != k) | (picked(b2, v2[k - 1]) != k) | (picked(cand, cv[k - 1]) != k) |
           repeats(v1) | repeats(v2) | repeats(cv))
    cat = lambda blocks: jnp.concatenate(blocks, axis=0)
    return (cat(r2).astype(BF16), cat(e2).astype(BF16), cat(ja), cat(cf)), tie.astype(jnp.int32)


def _route_kernel(ht_ref, wqt_ref, keys_ref, r2_ref, e2_ref, ja_ref, cf_ref, qt_scr):
    nk = PEER_N_KEYS
    tm = ht_ref.shape[1]
    qt_scr[...] = jnp.dot(wqt_ref[...], ht_ref[...], preferred_element_type=F32).astype(BF16)

    def head(h, carry):
        q0 = pl.multiple_of(h * 2 * nk, 2 * nk)
        s1 = jnp.dot(keys_ref[0], qt_scr[pl.ds(q0, nk), :], preferred_element_type=F32)
        s2 = jnp.dot(keys_ref[1], qt_scr[pl.ds(q0 + nk, nk), :], preferred_element_type=F32)
        def store(cols, res):
            r2_ref[h, :, cols], e2_ref[h, :, cols], ja_ref[h, :, cols], cf_ref[h, :, cols] = res

        for g in range(tm // LANES):
            cols = slice(g * LANES, (g + 1) * LANES)
            res, tie = _route_sorted(s1[:, cols], s2[:, cols])
            store(cols, res)

            @pl.when(jnp.max(tie) > 0)
            def _(cols=cols):
                store(cols, _route_exact(s1[:, cols], s2[:, cols]))
        return carry

    lax.fori_loop(0, PEER_HEADS, head, 0)


def _peer_route(ht, wqt, keys_bf, tm=512):
    d, t = ht.shape
    shp = lambda dt: jax.ShapeDtypeStruct((PEER_HEADS, PEER_N_KEYS, t), dt)
    spec = pl.BlockSpec((PEER_HEADS, PEER_N_KEYS, tm), lambda i: (0, 0, i))
    return pl.pallas_call(
        _route_kernel,
        grid=(t // tm,),
        in_specs=[pl.BlockSpec((d, tm), lambda i: (0, i)),
                  pl.BlockSpec(wqt.shape, lambda i: (0, 0)),
                  pl.BlockSpec(keys_bf.shape, lambda i: (0, 0, 0))],
        out_specs=[spec] * 4,
        out_shape=[shp(BF16), shp(BF16), shp(F32), shp(F32)],
        scratch_shapes=[pltpu.VMEM((wqt.shape[0], tm), BF16)],
        compiler_params=_cparams("parallel"),
        name="peer_route",
    )(ht, wqt, keys_bf)


def _cast_kernel(x_ref, o_ref, *, transpose):
    x = x_ref[...]
    o_ref[...] = (x.T if transpose else x).astype(BF16)


def _layer_bf16(w_all, layer, transpose, tr=512):
    _, r, c = w_all.shape
    return pl.pallas_call(
        functools.partial(_cast_kernel, transpose=transpose),
        grid=(r // tr,),
        in_specs=[pl.BlockSpec((None, tr, c), lambda i: (layer, i, 0))],
        out_specs=pl.BlockSpec((c, tr), lambda i: (0, i)) if transpose else pl.BlockSpec((tr, c), lambda i: (i, 0)),
        out_shape=jax.ShapeDtypeStruct((c, r) if transpose else (r, c), BF16),
        compiler_params=_cparams("parallel"),
        name="transpose_cast" if transpose else "cast",
    )(w_all)


EXP_CHUNK = 512
GATE_LANES = 256

def _gated_hidden(gel, a0, r2_ref, e2_ref, ja_ref, cf_ref, w_ref):
    nk = PEER_N_KEYS
    na = gel.shape[0] // nk
    zero = jnp.zeros((), BF16)
    tm = gel.shape[1]

    def row(ref, h, aa):
        return ref[h, pl.ds(a0 + aa, 1), :].astype(BF16)

    ja_rows = [[row(ja_ref, h, aa) for aa in range(na)] for h in range(PEER_HEADS)]
    cf_rows = [[row(cf_ref, h, aa) for aa in range(na)] for h in range(PEER_HEADS)]
    for lg in range(tm // GATE_LANES):
        lanes = slice(lg * GATE_LANES, (lg + 1) * GATE_LANES)
        acc = [None] * na
        for h in range(PEER_HEADS):
            r2c, e2c = r2_ref[h, :, lanes], e2_ref[h, :, lanes]
            for aa in range(na):
                term = jnp.where(r2c < ja_rows[h][aa][:, lanes], e2c * cf_rows[h][aa][:, lanes], zero)
                acc[aa] = term if acc[aa] is None else acc[aa] + term
        for aa in range(na):
            rows = slice(aa * nk, (aa + 1) * nk)
            w_ref[rows, lanes] = acc[aa] * gel[rows, lanes]


def _experts_kernel(ht_ref, u_ref, vt_ref, r2_ref, e2_ref, ja_ref, cf_ref, yt_ref, w_scr):
    j = pl.program_id(1)
    te = u_ref.shape[0]

    @pl.when(j == 0)
    def _():
        yt_ref[...] = jnp.zeros_like(yt_ref)

    ht = ht_ref[...]
    nchunk = te // EXP_CHUNK

    def hidden(c):
        return jnp.dot(u_ref[c * EXP_CHUNK:(c + 1) * EXP_CHUNK, :], ht, preferred_element_type=F32)

    act = hidden(0)
    for c in range(nchunk):
        nxt = hidden(c + 1) if c + 1 < nchunk else None
        gel = (0.5 * act * (1.0 + lax.erf(act * (2.0 ** -0.5)))).astype(BF16)
        a0 = j * (te // PEER_N_KEYS) + c * (EXP_CHUNK // PEER_N_KEYS)
        _gated_hidden(gel, a0, r2_ref, e2_ref, ja_ref, cf_ref, w_scr.at[c])
        yt_ref[...] += jnp.dot(vt_ref[:, c * EXP_CHUNK:(c + 1) * EXP_CHUNK], w_scr[c],
                               preferred_element_type=F32)
        act = nxt


def _peer_experts(ht, u_bf, vt_bf, r2, e2, ja, cf, tm=512, te=1024):
    d, t = ht.shape
    ne = u_bf.shape[0]
    side = pl.BlockSpec((PEER_HEADS, PEER_N_KEYS, tm), lambda i, j: (0, 0, i))
    return pl.pallas_call(
        _experts_kernel,
        grid=(t // tm, ne // te),
        in_specs=[pl.BlockSpec((d, tm), lambda i, j: (0, i)),
                  pl.BlockSpec((te, d), lambda i, j: (j, 0)),
                  pl.BlockSpec((d, te), lambda i, j: (0, j)),
                  side, side, side, side],
        out_specs=pl.BlockSpec((d, tm), lambda i, j: (0, i)),
        out_shape=jax.ShapeDtypeStruct((d, t), F32),
        scratch_shapes=[pltpu.VMEM((te // EXP_CHUNK, EXP_CHUNK, tm), BF16)],
        compiler_params=_cparams("parallel", "arbitrary"),
        name="peer_experts",
    )(ht, u_bf, vt_bf, r2, e2, ja, cf)


def _resid_ln_kernel(x_ref, yt_ref, gf_ref, lg_ref, lb_ref, o_ref, *, alpha):
    o_ref[...] = _layer_norm(alpha * x_ref[...] + (1.0 + gf_ref[0]) * yt_ref[...].T, lg_ref[...], lb_ref[...])


def _resid_ln(x2d, yt, mod3, ln_g, ln_b, seq, alpha, tm=512):
    t, d = x2d.shape
    per_b = seq // tm
    row = pl.BlockSpec((tm, d), lambda i: (i, 0))
    vec = pl.BlockSpec((1, d), lambda i: (0, 0))
    return pl.pallas_call(
        functools.partial(_resid_ln_kernel, alpha=alpha),
        grid=(t // tm,),
        in_specs=[row, pl.BlockSpec((d, tm), lambda i: (0, i)),
                  pl.BlockSpec((1, 1, d), lambda i: (i // per_b, 0, 5)), vec, vec],
        out_specs=row,
        out_shape=jax.ShapeDtypeStruct((t, d), F32),
        compiler_params=_cparams("parallel"),
        name="resid_ln",
    )(x2d, yt, mod3, ln_g.reshape(1, d), ln_b.reshape(1, d))


def kernel(x, c, w_mod, b_mod, w_in, a_q_gain, a_k_gain, c_sink, w_pa, w_pb, w_pc, w_o,
           ln1_g, ln1_b, peer_wq, peer_keys, peer_u, peer_v, ln2_g, ln2_b):
    nb, seq, d = x.shape
    depth = w_mod.shape[0]
    alpha = float((2 * depth) ** 0.25)
    ctab, stab = _rope_tables(seq)
    mod = _modulation(c, w_mod, b_mod)
    x2d = x.reshape(nb * seq, d)
    for l in range(depth):
        mod3 = mod[l].reshape(nb, 1, 6 * d)
        qkv_a, qkv_b, qkv_c, gl = _in_projection(x2d, mod3, w_in, l, seq)
        oa, u_bf, vt_bf = _mixer_a(qkv_a, ctab, stab, a_q_gain[l], a_k_gain[l], peer_u, peer_v, l, nb, seq)
        ob = _mixer_b(qkv_b, nb, seq)
        oc = _mixer_c(qkv_c, c_sink[l], nb, seq)
        x1, ht = _merge_project(oa, ob, oc, gl, x2d, mod3, ln1_g[l], ln1_b[l],
                                w_pa[l].astype(BF16), w_pb[l].astype(BF16), w_pc[l].astype(BF16),
                                w_o[l].astype(BF16), seq, alpha)
        r2, e2, ja, cf = _peer_route(ht, _layer_bf16(peer_wq, l, True), peer_keys[l].astype(BF16))
        yt = _peer_experts(ht, u_bf, vt_bf, r2, e2, ja, cf)
        x2d = _resid_ln(x1, yt, mod3, ln2_g[l], ln2_b[l], seq, alpha)
    return x2d.reshape(nb, seq, d)
```

```python
import functools

import numpy as np
import jax
import jax.numpy as jnp
from jax import lax
from jax.experimental import pallas as pl
from jax.experimental.pallas import tpu as pltpu

F32 = jnp.float32
BF16 = jnp.bfloat16

HEAD_DIM = 128
A_Q_HEADS, A_KV_HEADS = 8, 2
B_GROUPS = ((128, 1), (512, 4), (2048, 16))
B_HEADS_PER_GROUP = 2
B_HEADS = B_HEADS_PER_GROUP * len(B_GROUPS)
C_Q_HEADS, C_KV_HEADS = 8, 2
C_HALF_WINDOW = 128
N_BRANCHES = 3
GRID_W = 64
ROPE_THETA = 10000.0
PEER_HEADS = 8
PEER_N_KEYS = 128
PEER_TOPK = 16
LN_EPS = 1e-5
QK_EPS = 1e-6
NEG_INF = -1e30
MASK_DIST = 1e33

LANES = 128
SUBLANES = 8
BF16_ROWS = 16
VMEM_LIMIT = 56 * 1024 * 1024
MERGE_VMEM_LIMIT = 58 * 1024 * 1024


def _cparams(*sem, vmem=VMEM_LIMIT):
    return pltpu.CompilerParams(dimension_semantics=sem, vmem_limit_bytes=vmem)


def _mod_kernel(cb_ref, w_ref, b_ref, o_ref):
    nb, tn = cb_ref.shape[0], w_ref.shape[2]
    rows = []
    for b in range(nb):
        cb = cb_ref[b]
        pieces = [jnp.sum(w_ref[0, :, n * LANES:(n + 1) * LANES] * cb, axis=0, keepdims=True)
                  for n in range(tn // LANES)]
        rows.append(jnp.concatenate(pieces, axis=1))
    o_ref[0] = jnp.concatenate(rows, axis=0) + b_ref[0]


def _modulation(c, w_mod, b_mod, tn=1024):
    nl, d, n6 = w_mod.shape
    nb = c.shape[0]
    cb = jnp.broadcast_to(c[:, :, None], (nb, d, LANES))
    return pl.pallas_call(
        _mod_kernel,
        grid=(nl, n6 // tn),
        in_specs=[pl.BlockSpec((nb, d, LANES), lambda l, j: (0, 0, 0)),
                  pl.BlockSpec((1, d, tn), lambda l, j: (l, 0, j)),
                  pl.BlockSpec((1, 1, tn), lambda l, j: (l, 0, j))],
        out_specs=pl.BlockSpec((1, nb, tn), lambda l, j: (l, 0, j)),
        out_shape=jax.ShapeDtypeStruct((nl, nb, n6), F32),
        compiler_params=_cparams("parallel", "parallel"),
        name="adaln_modulation",
    )(cb, w_mod, b_mod.reshape(nl, 1, n6))


IN_TN = 768
IN_TILES = (2, 3, 2, 8)


def _inproj_kernel(x_ref, sc_ref, sh_ref, w_ref, oa_ref, ob_ref, oc_ref, og_ref, h_scr):
    j = pl.program_id(1)

    @pl.when(j == 0)
    def _():
        h_scr[...] = (x_ref[...] * (1.0 + sc_ref[0]) + sh_ref[0]).astype(BF16)

    lo = 0
    for ref, n in zip((oa_ref, ob_ref, oc_ref, og_ref), IN_TILES):
        @pl.when((j >= lo) & (j < lo + n))
        def _(ref=ref):
            ref[...] = jnp.dot(h_scr[...], w_ref[...].astype(BF16), preferred_element_type=F32).astype(BF16)
        lo += n


def _in_projection(x2d, mod3, w_all, layer, seq, tm=1024):
    t, d = x2d.shape
    tm = min(tm, seq)
    per_b = seq // tm
    starts = np.cumsum((0,) + IN_TILES[:-1]).tolist()

    def out_spec(lo, n):
        return pl.BlockSpec((tm, IN_TN), lambda i, j: (i, jnp.clip(j - lo, 0, n - 1)))

    return pl.pallas_call(
        _inproj_kernel,
        grid=(t // tm, sum(IN_TILES)),
        in_specs=[pl.BlockSpec((tm, d), lambda i, j: (i, 0)),
                  pl.BlockSpec((1, 1, d), lambda i, j: (i // per_b, 0, 1)),
                  pl.BlockSpec((1, 1, d), lambda i, j: (i // per_b, 0, 0)),
                  pl.BlockSpec((None, d, IN_TN), lambda i, j: (layer, 0, j))],
        out_specs=[out_spec(lo, n) for lo, n in zip(starts, IN_TILES)],
        out_shape=[jax.ShapeDtypeStruct((t, n * IN_TN), BF16) for n in IN_TILES],
        scratch_shapes=[pltpu.VMEM((tm, d), BF16)],
        compiler_params=_cparams("parallel", "arbitrary"),
        name="in_projection",
    )(x2d, mod3, mod3, w_all)


def _rope_tables(seq):
    rows = seq // GRID_W
    row = jnp.repeat(jnp.arange(rows, dtype=F32), GRID_W)
    col = jnp.tile(jnp.arange(GRID_W, dtype=F32), rows)
    quarter = HEAD_DIM // 4
    inv = ROPE_THETA ** (-jnp.arange(quarter, dtype=F32) / quarter)
    ar, ac = row[:, None] * inv, col[:, None] * inv
    ctab = jnp.concatenate([jnp.cos(ar), jnp.cos(ar), jnp.cos(ac), jnp.cos(ac)], axis=1)
    stab = jnp.concatenate([-jnp.sin(ar), jnp.sin(ar), -jnp.sin(ac), jnp.sin(ac)], axis=1)
    return ctab, stab


def _norm_rope(xf, gain, ctab, stab):
    hd = xf.shape[1]
    r = lax.broadcasted_iota(jnp.int32, (hd, hd), 0)
    c = lax.broadcasted_iota(jnp.int32, (hd, hd), 1)
    mean_mat = jnp.full((hd, hd), 1.0 / hd, BF16)
    swap_mat = jnp.where(r == (c ^ (hd // 4)), 1.0, 0.0).astype(BF16)
    ms = jnp.dot((xf * xf).astype(BF16), mean_mat, preferred_element_type=F32)
    xn = xf * lax.rsqrt(ms + QK_EPS) * gain
    partner = jnp.dot(xn.astype(BF16), swap_mat, preferred_element_type=F32)
    return xn * ctab + partner * stab


def _attn_a_kernel(q_ref, k_ref, v_ref, cq_ref, sq_ref, ck_ref, sk_ref, qg_ref, kg_ref, eu_ref, ev_ref,
                   o_ref, ub_ref, vt_ref, kp_scr, vp_scr, *, kc):
    tq, seq = q_ref.shape[0], k_ref.shape[0]
    grp = q_ref.shape[1] // HEAD_DIM
    ub_ref[...] = eu_ref[...].astype(BF16)
    vt_ref[...] = ev_ref[...].T.astype(BF16)

    @pl.when(pl.program_id(2) == 0)
    def _():
        kp_scr[...] = _norm_rope(k_ref[...].astype(F32), kg_ref[...], ck_ref[...], sk_ref[...]).astype(BF16)
        vp_scr[:, :HEAD_DIM] = v_ref[...]
        vp_scr[:, HEAD_DIM:] = jnp.ones((seq, HEAD_DIM), BF16)

    cq, sq, qg = cq_ref[...], sq_ref[...], qg_ref[...]
    scale = HEAD_DIM ** -0.5
    q4 = jnp.concatenate(
        [(_norm_rope(q_ref[:, h * HEAD_DIM:(h + 1) * HEAD_DIM].astype(F32), qg, cq, sq) * scale).astype(BF16)
         for h in range(grp)], axis=0)

    def scores(c):
        return lax.dot_general(q4, kp_scr[c * kc:(c + 1) * kc, :], (((1,), (1,)), ((), ())),
                               preferred_element_type=F32)

    nchunk = seq // kc
    s = scores(0)
    m = acc = None
    for c in range(nchunk):
        nxt = scores(c + 1) if c + 1 < nchunk else None
        smax = jnp.max(s, axis=-1, keepdims=True)
        mn = smax if m is None else jnp.maximum(m, smax)
        p = jnp.exp(s - mn).astype(BF16)
        pv = jnp.dot(p, vp_scr[c * kc:(c + 1) * kc, :], preferred_element_type=F32)
        acc = pv if m is None else jnp.exp(m - mn) * acc + pv
        m, s = mn, nxt
    o = acc[:, :HEAD_DIM] / acc[:, HEAD_DIM:HEAD_DIM + 1]
    for h in range(grp):
        o_ref[:, h * HEAD_DIM:(h + 1) * HEAD_DIM] = o[h * tq:(h + 1) * tq].astype(BF16)


def _mixer_a(qkv, ctab, stab, q_gain, k_gain, peer_u, peer_v, layer, nb, seq, tq=256, kc=2048):
    t = qkv.shape[0]
    nq = seq // tq
    kc = min(kc, seq)
    gw = (A_Q_HEADS // A_KV_HEADS) * HEAD_DIM
    k_blk = A_Q_HEADS
    v_blk = A_Q_HEADS + A_KV_HEADS
    _, ne, d = peer_u.shape
    slab = ne // (nb * A_KV_HEADS * nq)
    step = lambda b, g, i: (b * A_KV_HEADS + g) * nq + i
    expert_rows = pl.BlockSpec((None, slab, d), lambda b, g, i: (layer, step(b, g, i), 0))
    return pl.pallas_call(
        functools.partial(_attn_a_kernel, kc=kc),
        grid=(nb, A_KV_HEADS, nq),
        in_specs=[pl.BlockSpec((tq, gw), lambda b, g, i: (b * nq + i, g)),
                  pl.BlockSpec((seq, HEAD_DIM), lambda b, g, i: (b, k_blk + g)),
                  pl.BlockSpec((seq, HEAD_DIM), lambda b, g, i: (b, v_blk + g)),
                  pl.BlockSpec((tq, HEAD_DIM), lambda b, g, i: (i, 0)),
                  pl.BlockSpec((tq, HEAD_DIM), lambda b, g, i: (i, 0)),
                  pl.BlockSpec((seq, HEAD_DIM), lambda b, g, i: (0, 0)),
                  pl.BlockSpec((seq, HEAD_DIM), lambda b, g, i: (0, 0)),
                  pl.BlockSpec((1, HEAD_DIM), lambda b, g, i: (0, 0)),
                  pl.BlockSpec((1, HEAD_DIM), lambda b, g, i: (0, 0)),
                  expert_rows, expert_rows],
        out_specs=[pl.BlockSpec((tq, gw), lambda b, g, i: (b * nq + i, g)),
                   pl.BlockSpec((slab, d), lambda b, g, i: (step(b, g, i), 0)),
                   pl.BlockSpec((d, slab), lambda b, g, i: (0, step(b, g, i)))],
        out_shape=[jax.ShapeDtypeStruct((t, A_Q_HEADS * HEAD_DIM), BF16),
                   jax.ShapeDtypeStruct((ne, d), BF16), jax.ShapeDtypeStruct((d, ne), BF16)],
        scratch_shapes=[pltpu.VMEM((seq, HEAD_DIM), BF16), pltpu.VMEM((seq, 2 * HEAD_DIM), BF16)],
        compiler_params=_cparams("parallel", "parallel", "arbitrary"),
        name="mixer_a_axial_gqa",
    )(qkv, qkv, qkv, ctab, stab, ctab, stab, q_gain.reshape(1, HEAD_DIM), k_gain.reshape(1, HEAD_DIM),
      peer_u, peer_v)


def _window(q0, tq, half, seq):
    span = min(tq + 2 * half, seq)
    start = jnp.clip(q0 - half, 0, seq - span)
    return pl.multiple_of(start, 64), span


def _attn_b_kernel(q_ref, k_ref, v_ref, o_ref):
    tq, seq = q_ref.shape[0], k_ref.shape[0]
    q0 = pl.program_id(1) * tq
    scale = HEAD_DIM ** -0.5
    outs, lses = [], []
    for g, (win, r) in enumerate(B_GROUPS):
        half = win // 2
        start, span = _window(q0, tq, half, seq)
        qpos = q0 + lax.broadcasted_iota(jnp.int32, (tq, span), 0)
        kpos = start + lax.broadcasted_iota(jnp.int32, (tq, span), 1)
        dd = qpos - kpos
        ad = jnp.abs(dd)
        valid = (ad <= half) & ((dd & (r - 1)) == 0)
        dist = jnp.where(valid, ad.astype(F32), MASK_DIST)
        for hh in range(B_HEADS_PER_GROUP):
            head = g * B_HEADS_PER_GROUP + hh
            cols = slice(head * HEAD_DIM, (head + 1) * HEAD_DIM)
            slope = float(2.0 ** (-8.0 * (head + 1) / B_HEADS))
            qh = (q_ref[:, cols].astype(F32) * scale).astype(BF16)
            s = lax.dot_general(qh, k_ref[pl.ds(start, span), cols],
                                (((1,), (1,)), ((), ())), preferred_element_type=F32)
            s = s - slope * dist
            m = jnp.max(s, axis=-1, keepdims=True)
            p = jnp.exp(s - m)
            l = jnp.sum(p, axis=-1, keepdims=True)
            o = jnp.dot(p.astype(BF16), v_ref[pl.ds(start, span), cols], preferred_element_type=F32)
            outs.append(o / l)
            lses.append(m + jnp.log(l))
    ng = len(B_GROUPS)
    for hh in range(B_HEADS_PER_GROUP):
        hl = [lses[g * B_HEADS_PER_GROUP + hh] for g in range(ng)]
        mx = functools.reduce(jnp.maximum, hl)
        ex = [jnp.exp(x - mx) for x in hl]
        tot = functools.reduce(lambda a, b: a + b, ex)
        for g in range(ng):
            head = g * B_HEADS_PER_GROUP + hh
            o_ref[:, head * HEAD_DIM:(head + 1) * HEAD_DIM] = (outs[head] * (ex[g] / tot)).astype(BF16)


def _mixer_b(qkv, nb, seq, tq=256):
    t = qkv.shape[0]
    nq = seq // tq
    bw = B_HEADS * HEAD_DIM
    return pl.pallas_call(
        _attn_b_kernel,
        grid=(nb, nq),
        in_specs=[pl.BlockSpec((tq, bw), lambda b, i: (b * nq + i, 0)),
                  pl.BlockSpec((seq, bw), lambda b, i: (b, 1)),
                  pl.BlockSpec((seq, bw), lambda b, i: (b, 2))],
        out_specs=pl.BlockSpec((tq, bw), lambda b, i: (b * nq + i, 0)),
        out_shape=jax.ShapeDtypeStruct((t, bw), BF16),
        compiler_params=_cparams("parallel", "arbitrary"),
        name="mixer_b_dilated",
    )(qkv, qkv, qkv)


def _attn_c_kernel(hp_ref, q_ref, k_ref, v_ref, o_ref):
    tq, seq = q_ref.shape[0], k_ref.shape[0]
    grp = q_ref.shape[1] // HEAD_DIM
    g = pl.program_id(1)
    q0 = pl.program_id(2) * tq
    start, span = _window(q0, tq, C_HALF_WINDOW, seq)
    qpos = q0 + lax.broadcasted_iota(jnp.int32, (tq, span), 0)
    kpos = start + lax.broadcasted_iota(jnp.int32, (tq, span), 1)
    ad = jnp.abs(qpos - kpos)
    dist = jnp.where(ad <= C_HALF_WINDOW, ad.astype(F32), MASK_DIST)
    kk = k_ref[pl.ds(start, span), :]
    vv = v_ref[pl.ds(start, span), :]
    scale = HEAD_DIM ** -0.5
    for hh in range(grp):
        cols = slice(hh * HEAD_DIM, (hh + 1) * HEAD_DIM)
        slope, sink = hp_ref[g, hh], hp_ref[g, grp + hh]
        qh = (q_ref[:, cols].astype(F32) * scale).astype(BF16)
        s = lax.dot_general(qh, kk, (((1,), (1,)), ((), ())), preferred_element_type=F32)
        s = s - slope * dist
        m = jnp.maximum(jnp.max(s, axis=-1, keepdims=True), sink)
        p = jnp.exp(s - m)
        l = jnp.sum(p, axis=-1, keepdims=True) + jnp.exp(sink - m)
        o = jnp.dot(p.astype(BF16), vv, preferred_element_type=F32)
        o_ref[:, cols] = (o / l).astype(BF16)


def _mixer_c(qkv, sink, nb, seq, tq=256):
    t = qkv.shape[0]
    nq = seq // tq
    grp = C_Q_HEADS // C_KV_HEADS
    gw = grp * HEAD_DIM
    slopes = (2.0 ** (-8.0 * jnp.arange(1, C_Q_HEADS + 1, dtype=F32) / C_Q_HEADS)).reshape(C_KV_HEADS, grp)
    head_params = jnp.concatenate([slopes, sink.astype(F32)], axis=1)
    return pl.pallas_call(
        _attn_c_kernel,
        grid=(nb, C_KV_HEADS, nq),
        in_specs=[pl.BlockSpec(memory_space=pltpu.SMEM),
                  pl.BlockSpec((tq, gw), lambda b, g, i: (b * nq + i, g)),
                  pl.BlockSpec((seq, HEAD_DIM), lambda b, g, i: (b, C_Q_HEADS + g)),
                  pl.BlockSpec((seq, HEAD_DIM), lambda b, g, i: (b, C_Q_HEADS + C_KV_HEADS + g))],
        out_specs=pl.BlockSpec((tq, gw), lambda b, g, i: (b * nq + i, g)),
        out_shape=jax.ShapeDtypeStruct((t, C_Q_HEADS * HEAD_DIM), BF16),
        compiler_params=_cparams("parallel", "parallel", "arbitrary"),
        name="mixer_c_window_sink",
    )(head_params, qkv, qkv, qkv)


def _layer_norm(z, g, b):
    mu = jnp.mean(z, axis=-1, keepdims=True)
    zc = z - mu
    var = jnp.mean(zc * zc, axis=-1, keepdims=True)
    return zc * lax.rsqrt(var + LN_EPS) * g + b


def _branch_kernel(oa_ref, ob_ref, oc_ref, gl_ref, wa_ref, wb_ref, wc_ref, m_ref):
    d = m_ref.shape[1]
    merged = None
    for i, (o_ref, w_ref) in enumerate(((oa_ref, wa_ref), (ob_ref, wb_ref), (oc_ref, wc_ref))):
        y = jnp.dot(o_ref[...], w_ref[...], preferred_element_type=F32)
        term = jax.nn.sigmoid(gl_ref[:, i * d:(i + 1) * d].astype(F32)) * y
        merged = term if merged is None else merged + term
    m_ref[...] = merged.astype(BF16)


def _outproj_kernel(m_ref, x_ref, ga_ref, scf_ref, shf_ref, lg_ref, lb_ref, wo_ref, x1_ref, ht_ref, *, alpha):
    y = jnp.dot(m_ref[...], wo_ref[...], preferred_element_type=F32)
    x1 = _layer_norm(alpha * x_ref[...] + (1.0 + ga_ref[0]) * y, lg_ref[...], lb_ref[...])
    x1_ref[...] = x1
    h2 = x1 * (1.0 + scf_ref[0]) + shf_ref[0]
    ht_ref[...] = h2.T.astype(BF16)


def _merge_project(oa, ob, oc, gl, x2d, mod3, ln_g, ln_b, wa, wb, wc, wo, seq, alpha, tm=512):
    t, d = x2d.shape
    per_b = seq // tm
    row = lambda w: pl.BlockSpec((tm, w), lambda i: (i, 0))
    whole = lambda a: pl.BlockSpec(a.shape, lambda i: (0, 0), pipeline_mode=pl.Buffered(1))
    modspec = lambda k: pl.BlockSpec((1, 1, d), lambda i: (i // per_b, 0, k))
    vec = pl.BlockSpec((1, d), lambda i: (0, 0))
    merged = pl.pallas_call(
        _branch_kernel,
        grid=(t // tm,),
        in_specs=[row(oa.shape[1]), row(ob.shape[1]), row(oc.shape[1]), row(gl.shape[1]),
                  whole(wa), whole(wb), whole(wc)],
        out_specs=row(d),
        out_shape=jax.ShapeDtypeStruct((t, d), BF16),
        compiler_params=_cparams("parallel"),
        name="branch_merge",
    )(oa, ob, oc, gl, wa, wb, wc)
    return pl.pallas_call(
        functools.partial(_outproj_kernel, alpha=alpha),
        grid=(t // tm,),
        in_specs=[row(d), row(d), modspec(2), modspec(4), modspec(3), vec, vec, whole(wo)],
        out_specs=[pl.BlockSpec((tm, d), lambda i: (i, 0)), pl.BlockSpec((d, tm), lambda i: (0, i))],
        out_shape=[jax.ShapeDtypeStruct((t, d), F32), jax.ShapeDtypeStruct((d, t), BF16)],
        compiler_params=_cparams("parallel"),
        name="outproj_ln",
    )(merged, x2d, mod3, mod3, mod3, ln_g.reshape(1, d), ln_b.reshape(1, d), wo)


CAND_ROWS = 4


def _top_exact(s, n, order):
    nrow = lax.broadcasted_iota(jnp.int32, (n, s.shape[1]), 0)

    def step(i, carry):
        s, rank, vals = carry
        m = jnp.max(s, axis=0, keepdims=True)
        first = jnp.min(jnp.where(s == m, order, jnp.iinfo(jnp.int32).max), axis=0, keepdims=True)
        hit = order == first
        return jnp.where(hit, -jnp.inf, s), jnp.where(hit, i, rank), jnp.where(nrow == i, m, vals)

    _, rank, vals = lax.fori_loop(0, n, step, (s, jnp.full(s.shape, n, jnp.int32), jnp.zeros(nrow.shape, F32)))
    return vals, rank


def _candidates(v1, v2):
    k, c = PEER_TOPK, CAND_ROWS
    ki = lax.broadcasted_iota(jnp.int32, v1.shape, 0)
    v1_tail = jnp.where(ki >= c, v1, -jnp.inf)
    cand = jnp.concatenate([v1[i:i + 1] + v2 for i in range(c)] + [v2[j:j + 1] + v1_tail for j in range(c)], axis=0)
    flat = jnp.concatenate([ki + i * k for i in range(c)] +
                           [jnp.where(ki >= c, ki * k + j, -1) for j in range(c)], axis=0)
    return cand, flat


def _route_outputs(s1, s2, v1, v2, rank2, sel, is_ith):
    k, c = PEER_TOPK, CAND_ROWS
    e1 = jnp.exp(v1 - v1[0:1])
    e2 = jnp.exp(v2 - v2[0:1])
    prod = jnp.concatenate([e1[i:i + 1] * e2 for i in range(c)] + [e2[j:j + 1] * e1 for j in range(c)], axis=0)
    z = jnp.sum(jnp.where(sel, prod, 0.0), axis=0, keepdims=True)
    self32 = sel.astype(F32)
    tail = functools.reduce(lambda a, b: a + b, [self32[(c + j) * k:(c + j + 1) * k] for j in range(c)])
    ja = jnp.zeros(s1.shape, F32)
    for i in range(k):
        cnt = jnp.sum(self32[i * k:(i + 1) * k], axis=0, keepdims=True) if i < c else tail[i:i + 1]
        ja = jnp.where(is_ith(i), cnt, ja)
    return (rank2.astype(F32).astype(BF16), jnp.exp(s2 - v2[0:1]).astype(BF16), ja,
            jnp.exp(s1 - v1[0:1]) / z)


def _route_exact(s1, s2):
    k = PEER_TOPK
    rows = lax.broadcasted_iota(jnp.int32, s1.shape, 0)
    v1, rank1 = _top_exact(s1, k, rows)
    v2, rank2 = _top_exact(s2, k, rows)
    cand, flat = _candidates(v1, v2)
    _, crank = _top_exact(cand, k, flat)
    return _route_outputs(s1, s2, v1, v2, rank2, crank < k, lambda i: rank1 == i)


def _sort_network(n):
    def merge(lo, hi, r):
        step = r * 2
        if step < hi - lo:
            yield from merge(lo, hi, step)
            yield from merge(lo + r, hi, step)
            yield from [(i, i + r) for i in range(lo + r, hi - r, step)]
        else:
            yield (lo, lo + r)

    def sort(lo, hi):
        if hi - lo >= 1:
            mid = lo + (hi - lo) // 2
            yield from sort(lo, mid)
            yield from sort(mid + 1, hi)
            yield from merge(lo, hi, 1)

    return tuple(sort(0, n - 1))


def _rep_max(x):
    return jnp.broadcast_to(jnp.max(x, axis=0, keepdims=True), x.shape)


def _rep_sum(x):
    return jnp.broadcast_to(jnp.sum(x, axis=0, keepdims=True), x.shape)


def _top_sorted(groups, n):
    work = [list(g) for g in groups]
    for i, j in _sort_network(len(work[0])):
        for v in work:
            v[i], v[j] = jnp.maximum(v[i], v[j]), jnp.minimum(v[i], v[j])
    vals = [[] for _ in work]
    for it in range(n):
        for v, out in zip(work, vals):
            m = _rep_max(v[0])
            out.append(m)
            hit = v[0] == m
            for k in range(n - 1 - it):
                v[k] = jnp.where(hit, v[k + 1], v[k])
    return vals


def _route_sorted(s1, s2):
    k, c, sub = PEER_TOPK, CAND_ROWS, SUBLANES
    nblk = s1.shape[0] // sub
    b1 = [s1[sub * q:sub * (q + 1)] for q in range(nblk)]
    b2 = [s2[sub * q:sub * (q + 1)] for q in range(nblk)]
    v1, v2 = _top_sorted([b1, b2], k)
    srow = lax.broadcasted_iota(jnp.int32, b1[0].shape, 0)

    def stack(vals):
        return [functools.reduce(lambda acc, t: jnp.where(srow == t, vals[blk * sub + t], acc),
                                 range(1, sub), vals[blk * sub]) for blk in range(k // sub)]

    st1, st2 = stack(v1), stack(v2)
    st1_tail = [jnp.where(srow >= c, st1[0], -jnp.inf)] + st1[1:]
    cand = ([v1[i] + st2[b] for i in range(c) for b in range(k // sub)] +
            [v2[j] + st1_tail[b] for j in range(c) for b in range(k // sub)])
    cv = _top_sorted([cand], k)[0]
    sel = [x >= cv[k - 1] for x in cand]
    e1s = [jnp.exp(x - v1[0]) for x in st1]
    e2s = [jnp.exp(x - v2[0]) for x in st2]
    prod = ([jnp.exp(v1[i] - v1[0]) * e2s[b] for i in range(c) for b in range(k // sub)] +
            [jnp.exp(v2[j] - v2[0]) * e1s[b] for j in range(c) for b in range(k // sub)])
    z = _rep_sum(functools.reduce(lambda a, b: a + b, [jnp.where(s, p, 0.0) for s, p in zip(sel, prod)]))
    ones = [jnp.where(s, 1.0, 0.0) for s in sel]
    per = k // sub
    tail = [functools.reduce(lambda a, b: a + b, [ones[(c + j) * per + b] for j in range(c)]) for b in range(per)]
    counts = []
    for i in range(k):
        if i < c:
            counts.append(_rep_sum(functools.reduce(lambda a, b: a + b, ones[i * per:(i + 1) * per])))
        else:
            counts.append(_rep_sum(jnp.where(srow == i % sub, tail[i // sub], 0.0)))
    inv_z = 1.0 / z
    ja, cf, r2, e2 = [], [], [], []
    for q in range(nblk):
        jq = jnp.zeros(b1[q].shape, F32)
        rq = jnp.zeros(b2[q].shape, F32)
        for i in range(k):
            jq = jnp.where(b1[q] == v1[i], counts[i], jq)
            rq = jnp.where(b2[q] < v2[i], float(i + 1), rq)
        ja.append(jq)
        r2.append(rq)
        cf.append(jnp.exp(b1[q] - v1[0]) * inv_z)
        e2.append(jnp.exp(b2[q] - v2[0]))

    def picked(blocks, thr):
        return _rep_sum(functools.reduce(lambda a, b: a + b, [jnp.where(x >= thr, 1.0, 0.0) for x in blocks]))

    def repeats(vals):
        return functools.reduce(lambda a, b: a | b, [vals[i] == vals[i + 1] for i in range(k - 1)])

    tie = ((picked(b1, v1[k - 1]) != k) | (picked(b2, v2[k - 1]) != k) | (picked(cand, cv[k - 1]) != k) |
           repeats(v1) | repeats(v2) | repeats(cv))
    cat = lambda blocks: jnp.concatenate(blocks, axis=0)
    return (cat(r2).astype(BF16), cat(e2).astype(BF16), cat(ja), cat(cf)), tie.astype(jnp.int32)


def _route_kernel(ht_ref, wqt_ref, keys_ref, r2_ref, e2_ref, ja_ref, cf_ref, qt_scr):
    nk = PEER_N_KEYS
    tm = ht_ref.shape[1]
    qt_scr[...] = jnp.dot(wqt_ref[...], ht_ref[...], preferred_element_type=F32).astype(BF16)

    def head(h, carry):
        q0 = pl.multiple_of(h * 2 * nk, 2 * nk)
        s1 = jnp.dot(keys_ref[0], qt_scr[pl.ds(q0, nk), :], preferred_element_type=F32)
        s2 = jnp.dot(keys_ref[1], qt_scr[pl.ds(q0 + nk, nk), :], preferred_element_type=F32)
        def store(cols, res):
            r2_ref[h, :, cols], e2_ref[h, :, cols], ja_ref[h, :, cols], cf_ref[h, :, cols] = res

        for g in range(tm // LANES):
            cols = slice(g * LANES, (g + 1) * LANES)
            res, tie = _route_sorted(s1[:, cols], s2[:, cols])
            store(cols, res)

            @pl.when(jnp.max(tie) > 0)
            def _(cols=cols):
                store(cols, _route_exact(s1[:, cols], s2[:, cols]))
        return carry

    lax.fori_loop(0, PEER_HEADS, head, 0)


def _peer_route(ht, wqt, keys_bf, tm=512):
    d, t = ht.shape
    shp = lambda dt: jax.ShapeDtypeStruct((PEER_HEADS, PEER_N_KEYS, t), dt)
    spec = pl.BlockSpec((PEER_HEADS, PEER_N_KEYS, tm), lambda i: (0, 0, i))
    return pl.pallas_call(
        _route_kernel,
        grid=(t // tm,),
        in_specs=[pl.BlockSpec((d, tm), lambda i: (0, i)),
                  pl.BlockSpec(wqt.shape, lambda i: (0, 0)),
                  pl.BlockSpec(keys_bf.shape, lambda i: (0, 0, 0))],
        out_specs=[spec] * 4,
        out_shape=[shp(BF16), shp(BF16), shp(F32), shp(F32)],
        scratch_shapes=[pltpu.VMEM((wqt.shape[0], tm), BF16)],
        compiler_params=_cparams("parallel"),
        name="peer_route",
    )(ht, wqt, keys_bf)


def _cast_kernel(x_ref, o_ref, *, transpose):
    x = x_ref[...]
    o_ref[...] = (x.T if transpose else x).astype(BF16)


def _layer_bf16(w_all, layer, transpose, tr=512):
    _, r, c = w_all.shape
    return pl.pallas_call(
        functools.partial(_cast_kernel, transpose=transpose),
        grid=(r // tr,),
        in_specs=[pl.BlockSpec((None, tr, c), lambda i: (layer, i, 0))],
        out_specs=pl.BlockSpec((c, tr), lambda i: (0, i)) if transpose else pl.BlockSpec((tr, c), lambda i: (i, 0)),
        out_shape=jax.ShapeDtypeStruct((c, r) if transpose else (r, c), BF16),
        compiler_params=_cparams("parallel"),
        name="transpose_cast" if transpose else "cast",
    )(w_all)


EXP_CHUNK = 512
GATE_LANES = 256

def _gated_hidden(gel, a0, r2_ref, e2_ref, ja_ref, cf_ref, w_ref):
    nk = PEER_N_KEYS
    na = gel.shape[0] // nk
    zero = jnp.zeros((), BF16)
    tm = gel.shape[1]

    def row(ref, h, aa):
        return ref[h, pl.ds(a0 + aa, 1), :].astype(BF16)

    ja_rows = [[row(ja_ref, h, aa) for aa in range(na)] for h in range(PEER_HEADS)]
    cf_rows = [[row(cf_ref, h, aa) for aa in range(na)] for h in range(PEER_HEADS)]
    for lg in range(tm // GATE_LANES):
        lanes = slice(lg * GATE_LANES, (lg + 1) * GATE_LANES)
        acc = [None] * na
        for h in range(PEER_HEADS):
            r2c, e2c = r2_ref[h, :, lanes], e2_ref[h, :, lanes]
            for aa in range(na):
                term = jnp.where(r2c < ja_rows[h][aa][:, lanes], e2c * cf_rows[h][aa][:, lanes], zero)
                acc[aa] = term if acc[aa] is None else acc[aa] + term
        for aa in range(na):
            rows = slice(aa * nk, (aa + 1) * nk)
            w_ref[rows, lanes] = acc[aa] * gel[rows, lanes]


def _experts_kernel(ht_ref, u_ref, vt_ref, r2_ref, e2_ref, ja_ref, cf_ref, yt_ref, w_scr):
    j = pl.program_id(1)
    te = u_ref.shape[0]

    @pl.when(j == 0)
    def _():
        yt_ref[...] = jnp.zeros_like(yt_ref)

    ht = ht_ref[...]
    nchunk = te // EXP_CHUNK

    def hidden(c):
        return jnp.dot(u_ref[c * EXP_CHUNK:(c + 1) * EXP_CHUNK, :], ht, preferred_element_type=F32)

    act = hidden(0)
    for c in range(nchunk):
        nxt = hidden(c + 1) if c + 1 < nchunk else None
        gel = (0.5 * act * (1.0 + lax.erf(act * (2.0 ** -0.5)))).astype(BF16)
        a0 = j * (te // PEER_N_KEYS) + c * (EXP_CHUNK // PEER_N_KEYS)
        _gated_hidden(gel, a0, r2_ref, e2_ref, ja_ref, cf_ref, w_scr.at[c])
        yt_ref[...] += jnp.dot(vt_ref[:, c * EXP_CHUNK:(c + 1) * EXP_CHUNK], w_scr[c],
                               preferred_element_type=F32)
        act = nxt


def _peer_experts(ht, u_bf, vt_bf, r2, e2, ja, cf, tm=512, te=1024):
    d, t = ht.shape
    ne = u_bf.shape[0]
    side = pl.BlockSpec((PEER_HEADS, PEER_N_KEYS, tm), lambda i, j: (0, 0, i))
    return pl.pallas_call(
        _experts_kernel,
        grid=(t // tm, ne // te),
        in_specs=[pl.BlockSpec((d, tm), lambda i, j: (0, i)),
                  pl.BlockSpec((te, d), lambda i, j: (j, 0)),
                  pl.BlockSpec((d, te), lambda i, j: (0, j)),
                  side, side, side, side],
        out_specs=pl.BlockSpec((d, tm), lambda i, j: (0, i)),
        out_shape=jax.ShapeDtypeStruct((d, t), F32),
        scratch_shapes=[pltpu.VMEM((te // EXP_CHUNK, EXP_CHUNK, tm), BF16)],
        compiler_params=_cparams("parallel", "arbitrary"),
        name="peer_experts",
    )(ht, u_bf, vt_bf, r2, e2, ja, cf)


def _resid_ln_kernel(x_ref, yt_ref, gf_ref, lg_ref, lb_ref, o_ref, *, alpha):
    o_ref[...] = _layer_norm(alpha * x_ref[...] + (1.0 + gf_ref[0]) * yt_ref[...].T, lg_ref[...], lb_ref[...])


def _resid_ln(x2d, yt, mod3, ln_g, ln_b, seq, alpha, tm=512):
    t, d = x2d.shape
    per_b = seq // tm
    row = pl.BlockSpec((tm, d), lambda i: (i, 0))
    vec = pl.BlockSpec((1, d), lambda i: (0, 0))
    return pl.pallas_call(
        functools.partial(_resid_ln_kernel, alpha=alpha),
        grid=(t // tm,),
        in_specs=[row, pl.BlockSpec((d, tm), lambda i: (0, i)),
                  pl.BlockSpec((1, 1, d), lambda i: (i // per_b, 0, 5)), vec, vec],
        out_specs=row,
        out_shape=jax.ShapeDtypeStruct((t, d), F32),
        compiler_params=_cparams("parallel"),
        name="resid_ln",
    )(x2d, yt, mod3, ln_g.reshape(1, d), ln_b.reshape(1, d))


def kernel(x, c, w_mod, b_mod, w_in, a_q_gain, a_k_gain, c_sink, w_pa, w_pb, w_pc, w_o,
           ln1_g, ln1_b, peer_wq, peer_keys, peer_u, peer_v, ln2_g, ln2_b):
    nb, seq, d = x.shape
    depth = w_mod.shape[0]
    alpha = float((2 * depth) ** 0.25)
    ctab, stab = _rope_tables(seq)
    mod = _modulation(c, w_mod, b_mod)
    x2d = x.reshape(nb * seq, d)
    for l in range(depth):
        mod3 = mod[l].reshape(nb, 1, 6 * d)
        qkv_a, qkv_b, qkv_c, gl = _in_projection(x2d, mod3, w_in, l, seq)
        oa, u_bf, vt_bf = _mixer_a(qkv_a, ctab, stab, a_q_gain[l], a_k_gain[l], peer_u, peer_v, l, nb, seq)
        ob = _mixer_b(qkv_b, nb, seq)
        oc = _mixer_c(qkv_c, c_sink[l], nb, seq)
        x1, ht = _merge_project(oa, ob, oc, gl, x2d, mod3, ln1_g[l], ln1_b[l],
                                w_pa[l].astype(BF16), w_pb[l].astype(BF16), w_pc[l].astype(BF16),
                                w_o[l].astype(BF16), seq, alpha)
        r2, e2, ja, cf = _peer_route(ht, _layer_bf16(peer_wq, l, True), peer_keys[l].astype(BF16))
        yt = _peer_experts(ht, u_bf, vt_bf, r2, e2, ja, cf)
        x2d = _resid_ln(x1, yt, mod3, ln2_g[l], ln2_b[l], seq, alpha)
    return x2d.reshape(nb, seq, d)
```

```python
import functools

import numpy as np
import jax
import jax.numpy as jnp
from jax import lax
from jax.experimental import pallas as pl
from jax.experimental.pallas import tpu as pltpu

F32 = jnp.float32
BF16 = jnp.bfloat16

HEAD_DIM = 128
A_Q_HEADS, A_KV_HEADS = 8, 2
B_GROUPS = ((128, 1), (512, 4), (2048, 16))
B_HEADS_PER_GROUP = 2
B_HEADS = B_HEADS_PER_GROUP * len(B_GROUPS)
C_Q_HEADS, C_KV_HEADS = 8, 2
C_HALF_WINDOW = 128
N_BRANCHES = 3
GRID_W = 64
ROPE_THETA = 10000.0
PEER_HEADS = 8
PEER_N_KEYS = 128
PEER_TOPK = 16
LN_EPS = 1e-5
QK_EPS = 1e-6
NEG_INF = -1e30
MASK_DIST = 1e33

LANES = 128
SUBLANES = 8
BF16_ROWS = 16
VMEM_LIMIT = 56 * 1024 * 1024
MERGE_VMEM_LIMIT = 58 * 1024 * 1024


def _cparams(*sem, vmem=VMEM_LIMIT):
    return pltpu.CompilerParams(dimension_semantics=sem, vmem_limit_bytes=vmem)


def _mod_kernel(cb_ref, w_ref, b_ref, o_ref):
    nb, tn = cb_ref.shape[0], w_ref.shape[2]
    rows = []
    for b in range(nb):
        cb = cb_ref[b]
        pieces = [jnp.sum(w_ref[0, :, n * LANES:(n + 1) * LANES] * cb, axis=0, keepdims=True)
                  for n in range(tn // LANES)]
        rows.append(jnp.concatenate(pieces, axis=1))
    o_ref[0] = jnp.concatenate(rows, axis=0) + b_ref[0]


def _modulation(c, w_mod, b_mod, tn=1024):
    nl, d, n6 = w_mod.shape
    nb = c.shape[0]
    cb = jnp.broadcast_to(c[:, :, None], (nb, d, LANES))
    return pl.pallas_call(
        _mod_kernel,
        grid=(nl, n6 // tn),
        in_specs=[pl.BlockSpec((nb, d, LANES), lambda l, j: (0, 0, 0)),
                  pl.BlockSpec((1, d, tn), lambda l, j: (l, 0, j)),
                  pl.BlockSpec((1, 1, tn), lambda l, j: (l, 0, j))],
        out_specs=pl.BlockSpec((1, nb, tn), lambda l, j: (l, 0, j)),
        out_shape=jax.ShapeDtypeStruct((nl, nb, n6), F32),
        compiler_params=_cparams("parallel", "parallel"),
        name="adaln_modulation",
    )(cb, w_mod, b_mod.reshape(nl, 1, n6))


IN_TN = 768
IN_TILES = (2, 3, 2, 8)


def _inproj_kernel(x_ref, sc_ref, sh_ref, w_ref, oa_ref, ob_ref, oc_ref, og_ref, h_scr):
    j = pl.program_id(1)

    @pl.when(j == 0)
    def _():
        h_scr[...] = (x_ref[...] * (1.0 + sc_ref[0]) + sh_ref[0]).astype(BF16)

    lo = 0
    for ref, n in zip((oa_ref, ob_ref, oc_ref, og_ref), IN_TILES):
        @pl.when((j >= lo) & (j < lo + n))
        def _(ref=ref):
            ref[...] = jnp.dot(h_scr[...], w_ref[...].astype(BF16), preferred_element_type=F32).astype(BF16)
        lo += n


def _in_projection(x2d, mod3, w_all, layer, seq, tm=1024):
    t, d = x2d.shape
    tm = min(tm, seq)
    per_b = seq // tm
    starts = np.cumsum((0,) + IN_TILES[:-1]).tolist()

    def out_spec(lo, n):
        return pl.BlockSpec((tm, IN_TN), lambda i, j: (i, jnp.clip(j - lo, 0, n - 1)))

    return pl.pallas_call(
        _inproj_kernel,
        grid=(t // tm, sum(IN_TILES)),
        in_specs=[pl.BlockSpec((tm, d), lambda i, j: (i, 0)),
                  pl.BlockSpec((1, 1, d), lambda i, j: (i // per_b, 0, 1)),
                  pl.BlockSpec((1, 1, d), lambda i, j: (i // per_b, 0, 0)),
                  pl.BlockSpec((None, d, IN_TN), lambda i, j: (layer, 0, j))],
        out_specs=[out_spec(lo, n) for lo, n in zip(starts, IN_TILES)],
        out_shape=[jax.ShapeDtypeStruct((t, n * IN_TN), BF16) for n in IN_TILES],
        scratch_shapes=[pltpu.VMEM((tm, d), BF16)],
        compiler_params=_cparams("parallel", "arbitrary"),
        name="in_projection",
    )(x2d, mod3, mod3, w_all)


def _rope_tables(seq):
    rows = seq // GRID_W
    row = jnp.repeat(jnp.arange(rows, dtype=F32), GRID_W)
    col = jnp.tile(jnp.arange(GRID_W, dtype=F32), rows)
    quarter = HEAD_DIM // 4
    inv = ROPE_THETA ** (-jnp.arange(quarter, dtype=F32) / quarter)
    ar, ac = row[:, None] * inv, col[:, None] * inv
    ctab = jnp.concatenate([jnp.cos(ar), jnp.cos(ar), jnp.cos(ac), jnp.cos(ac)], axis=1)
    stab = jnp.concatenate([-jnp.sin(ar), jnp.sin(ar), -jnp.sin(ac), jnp.sin(ac)], axis=1)
    return ctab, stab


def _norm_rope(xf, gain, ctab, stab):
    hd = xf.shape[1]
    r = lax.broadcasted_iota(jnp.int32, (hd, hd), 0)
    c = lax.broadcasted_iota(jnp.int32, (hd, hd), 1)
    mean_mat = jnp.full((hd, hd), 1.0 / hd, BF16)
    swap_mat = jnp.where(r == (c ^ (hd // 4)), 1.0, 0.0).astype(BF16)
    ms = jnp.dot((xf * xf).astype(BF16), mean_mat, preferred_element_type=F32)
    xn = xf * lax.rsqrt(ms + QK_EPS) * gain
    partner = jnp.dot(xn.astype(BF16), swap_mat, preferred_element_type=F32)
    return xn * ctab + partner * stab


def _attn_a_kernel(q_ref, k_ref, v_ref, cq_ref, sq_ref, ck_ref, sk_ref, qg_ref, kg_ref, eu_ref, ev_ref,
                   o_ref, ub_ref, vt_ref, kp_scr, vp_scr, *, kc):
    tq, seq = q_ref.shape[0], k_ref.shape[0]
    grp = q_ref.shape[1] // HEAD_DIM
    ub_ref[...] = eu_ref[...].astype(BF16)
    vt_ref[...] = ev_ref[...].T.astype(BF16)

    @pl.when(pl.program_id(2) == 0)
    def _():
        kp_scr[...] = _norm_rope(k_ref[...].astype(F32), kg_ref[...], ck_ref[...], sk_ref[...]).astype(BF16)
        vp_scr[:, :HEAD_DIM] = v_ref[...]
        vp_scr[:, HEAD_DIM:] = jnp.ones((seq, HEAD_DIM), BF16)

    cq, sq, qg = cq_ref[...], sq_ref[...], qg_ref[...]
    scale = HEAD_DIM ** -0.5
    q4 = jnp.concatenate(
        [(_norm_rope(q_ref[:, h * HEAD_DIM:(h + 1) * HEAD_DIM].astype(F32), qg, cq, sq) * scale).astype(BF16)
         for h in range(grp)], axis=0)

    def scores(c):
        return lax.dot_general(q4, kp_scr[c * kc:(c + 1) * kc, :], (((1,), (1,)), ((), ())),
                               preferred_element_type=F32)

    nchunk = seq // kc
    s = scores(0)
    m = acc = None
    for c in range(nchunk):
        nxt = scores(c + 1) if c + 1 < nchunk else None
        smax = jnp.max(s, axis=-1, keepdims=True)
        mn = smax if m is None else jnp.maximum(m, smax)
        p = jnp.exp(s - mn).astype(BF16)
        pv = jnp.dot(p, vp_scr[c * kc:(c + 1) * kc, :], preferred_element_type=F32)
        acc = pv if m is None else jnp.exp(m - mn) * acc + pv
        m, s = mn, nxt
    o = acc[:, :HEAD_DIM] / acc[:, HEAD_DIM:HEAD_DIM + 1]
    for h in range(grp):
        o_ref[:, h * HEAD_DIM:(h + 1) * HEAD_DIM] = o[h * tq:(h + 1) * tq].astype(BF16)


def _mixer_a(qkv, ctab, stab, q_gain, k_gain, peer_u, peer_v, layer, nb, seq, tq=256, kc=2048):
    t = qkv.shape[0]
    nq = seq // tq
    kc = min(kc, seq)
    gw = (A_Q_HEADS // A_KV_HEADS) * HEAD_DIM
    k_blk = A_Q_HEADS
    v_blk = A_Q_HEADS + A_KV_HEADS
    _, ne, d = peer_u.shape
    slab = ne // (nb * A_KV_HEADS * nq)
    step = lambda b, g, i: (b * A_KV_HEADS + g) * nq + i
    expert_rows = pl.BlockSpec((None, slab, d), lambda b, g, i: (layer, step(b, g, i), 0))
    return pl.pallas_call(
        functools.partial(_attn_a_kernel, kc=kc),
        grid=(nb, A_KV_HEADS, nq),
        in_specs=[pl.BlockSpec((tq, gw), lambda b, g, i: (b * nq + i, g)),
                  pl.BlockSpec((seq, HEAD_DIM), lambda b, g, i: (b, k_blk + g)),
                  pl.BlockSpec((seq, HEAD_DIM), lambda b, g, i: (b, v_blk + g)),
                  pl.BlockSpec((tq, HEAD_DIM), lambda b, g, i: (i, 0)),
                  pl.BlockSpec((tq, HEAD_DIM), lambda b, g, i: (i, 0)),
                  pl.BlockSpec((seq, HEAD_DIM), lambda b, g, i: (0, 0)),
                  pl.BlockSpec((seq, HEAD_DIM), lambda b, g, i: (0, 0)),
                  pl.BlockSpec((1, HEAD_DIM), lambda b, g, i: (0, 0)),
                  pl.BlockSpec((1, HEAD_DIM), lambda b, g, i: (0, 0)),
                  expert_rows, expert_rows],
        out_specs=[pl.BlockSpec((tq, gw), lambda b, g, i: (b * nq + i, g)),
                   pl.BlockSpec((slab, d), lambda b, g, i: (step(b, g, i), 0)),
                   pl.BlockSpec((d, slab), lambda b, g, i: (0, step(b, g, i)))],
        out_shape=[jax.ShapeDtypeStruct((t, A_Q_HEADS * HEAD_DIM), BF16),
                   jax.ShapeDtypeStruct((ne, d), BF16), jax.ShapeDtypeStruct((d, ne), BF16)],
        scratch_shapes=[pltpu.VMEM((seq, HEAD_DIM), BF16), pltpu.VMEM((seq, 2 * HEAD_DIM), BF16)],
        compiler_params=_cparams("parallel", "parallel", "arbitrary"),
        name="mixer_a_axial_gqa",
    )(qkv, qkv, qkv, ctab, stab, ctab, stab, q_gain.reshape(1, HEAD_DIM), k_gain.reshape(1, HEAD_DIM),
      peer_u, peer_v)


def _window(q0, tq, half, seq):
    span = min(tq + 2 * half, seq)
    start = jnp.clip(q0 - half, 0, seq - span)
    return pl.multiple_of(start, 64), span


def _attn_b_kernel(q_ref, k_ref, v_ref, o_ref):
    tq, seq = q_ref.shape[0], k_ref.shape[0]
    q0 = pl.program_id(1) * tq
    scale = HEAD_DIM ** -0.5
    outs, lses = [], []
    for g, (win, r) in enumerate(B_GROUPS):
        half = win // 2
        start, span = _window(q0, tq, half, seq)
        qpos = q0 + lax.broadcasted_iota(jnp.int32, (tq, span), 0)
        kpos = start + lax.broadcasted_iota(jnp.int32, (tq, span), 1)
        dd = qpos - kpos
        ad = jnp.abs(dd)
        valid = (ad <= half) & ((dd & (r - 1)) == 0)
        dist = jnp.where(valid, ad.astype(F32), MASK_DIST)
        for hh in range(B_HEADS_PER_GROUP):
            head = g * B_HEADS_PER_GROUP + hh
            cols = slice(head * HEAD_DIM, (head + 1) * HEAD_DIM)
            slope = float(2.0 ** (-8.0 * (head + 1) / B_HEADS))
            qh = (q_ref[:, cols].astype(F32) * scale).astype(BF16)
            s = lax.dot_general(qh, k_ref[pl.ds(start, span), cols],
                                (((1,), (1,)), ((), ())), preferred_element_type=F32)
            s = s - slope * dist
            m = jnp.max(s, axis=-1, keepdims=True)
            p = jnp.exp(s - m)
            l = jnp.sum(p, axis=-1, keepdims=True)
            o = jnp.dot(p.astype(BF16), v_ref[pl.ds(start, span), cols], preferred_element_type=F32)
            outs.append(o / l)
            lses.append(m + jnp.log(l))
    ng = len(B_GROUPS)
    for hh in range(B_HEADS_PER_GROUP):
        hl = [lses[g * B_HEADS_PER_GROUP + hh] for g in range(ng)]
        mx = functools.reduce(jnp.maximum, hl)
        ex = [jnp.exp(x - mx) for x in hl]
        tot = functools.reduce(lambda a, b: a + b, ex)
        for g in range(ng):
            head = g * B_HEADS_PER_GROUP + hh
            o_ref[:, head * HEAD_DIM:(head + 1) * HEAD_DIM] = (outs[head] * (ex[g] / tot)).astype(BF16)


def _mixer_b(qkv, nb, seq, tq=256):
    t = qkv.shape[0]
    nq = seq // tq
    bw = B_HEADS * HEAD_DIM
    return pl.pallas_call(
        _attn_b_kernel,
        grid=(nb, nq),
        in_specs=[pl.BlockSpec((tq, bw), lambda b, i: (b * nq + i, 0)),
                  pl.BlockSpec((seq, bw), lambda b, i: (b, 1)),
                  pl.BlockSpec((seq, bw), lambda b, i: (b, 2))],
        out_specs=pl.BlockSpec((tq, bw), lambda b, i: (b * nq + i, 0)),
        out_shape=jax.ShapeDtypeStruct((t, bw), BF16),
        compiler_params=_cparams("parallel", "arbitrary"),
        name="mixer_b_dilated",
    )(qkv, qkv, qkv)


def _attn_c_kernel(hp_ref, q_ref, k_ref, v_ref, o_ref):
    tq, seq = q_ref.shape[0], k_ref.shape[0]
    grp = q_ref.shape[1] // HEAD_DIM
    g = pl.program_id(1)
    q0 = pl.program_id(2) * tq
    start, span = _window(q0, tq, C_HALF_WINDOW, seq)
    qpos = q0 + lax.broadcasted_iota(jnp.int32, (tq, span), 0)
    kpos = start + lax.broadcasted_iota(jnp.int32, (tq, span), 1)
    ad = jnp.abs(qpos - kpos)
    dist = jnp.where(ad <= C_HALF_WINDOW, ad.astype(F32), MASK_DIST)
    kk = k_ref[pl.ds(start, span), :]
    vv = v_ref[pl.ds(start, span), :]
    scale = HEAD_DIM ** -0.5
    for hh in range(grp):
        cols = slice(hh * HEAD_DIM, (hh + 1) * HEAD_DIM)
        slope, sink = hp_ref[g, hh], hp_ref[g, grp + hh]
        qh = (q_ref[:, cols].astype(F32) * scale).astype(BF16)
        s = lax.dot_general(qh, kk, (((1,), (1,)), ((), ())), preferred_element_type=F32)
        s = s - slope * dist
        m = jnp.maximum(jnp.max(s, axis=-1, keepdims=True), sink)
        p = jnp.exp(s - m)
        l = jnp.sum(p, axis=-1, keepdims=True) + jnp.exp(sink - m)
        o = jnp.dot(p.astype(BF16), vv, preferred_element_type=F32)
        o_ref[:, cols] = (o / l).astype(BF16)


def _mixer_c(qkv, sink, nb, seq, tq=256):
    t = qkv.shape[0]
    nq = seq // tq
    grp = C_Q_HEADS // C_KV_HEADS
    gw = grp * HEAD_DIM
    slopes = (2.0 ** (-8.0 * jnp.arange(1, C_Q_HEADS + 1, dtype=F32) / C_Q_HEADS)).reshape(C_KV_HEADS, grp)
    head_params = jnp.concatenate([slopes, sink.astype(F32)], axis=1)
    return pl.pallas_call(
        _attn_c_kernel,
        grid=(nb, C_KV_HEADS, nq),
        in_specs=[pl.BlockSpec(memory_space=pltpu.SMEM),
                  pl.BlockSpec((tq, gw), lambda b, g, i: (b * nq + i, g)),
                  pl.BlockSpec((seq, HEAD_DIM), lambda b, g, i: (b, C_Q_HEADS + g)),
                  pl.BlockSpec((seq, HEAD_DIM), lambda b, g, i: (b, C_Q_HEADS + C_KV_HEADS + g))],
        out_specs=pl.BlockSpec((tq, gw), lambda b, g, i: (b * nq + i, g)),
        out_shape=jax.ShapeDtypeStruct((t, C_Q_HEADS * HEAD_DIM), BF16),
        compiler_params=_cparams("parallel", "parallel", "arbitrary"),
        name="mixer_c_window_sink",
    )(head_params, qkv, qkv, qkv)


def _layer_norm(z, g, b):
    mu = jnp.mean(z, axis=-1, keepdims=True)
    zc = z - mu
    var = jnp.mean(zc * zc, axis=-1, keepdims=True)
    return zc * lax.rsqrt(var + LN_EPS) * g + b


def _branch_kernel(oa_ref, ob_ref, oc_ref, gl_ref, wa_ref, wb_ref, wc_ref, m_ref):
    d = m_ref.shape[1]
    merged = None
    for i, (o_ref, w_ref) in enumerate(((oa_ref, wa_ref), (ob_ref, wb_ref), (oc_ref, wc_ref))):
        y = jnp.dot(o_ref[...], w_ref[...], preferred_element_type=F32)
        term = jax.nn.sigmoid(gl_ref[:, i * d:(i + 1) * d].astype(F32)) * y
        merged = term if merged is None else merged + term
    m_ref[...] = merged.astype(BF16)


def _outproj_kernel(m_ref, x_ref, ga_ref, scf_ref, shf_ref, lg_ref, lb_ref, wo_ref, x1_ref, ht_ref, *, alpha):
    y = jnp.dot(m_ref[...], wo_ref[...], preferred_element_type=F32)
    x1 = _layer_norm(alpha * x_ref[...] + (1.0 + ga_ref[0]) * y, lg_ref[...], lb_ref[...])
    x1_ref[...] = x1
    h2 = x1 * (1.0 + scf_ref[0]) + shf_ref[0]
    ht_ref[...] = h2.T.astype(BF16)


def _merge_project(oa, ob, oc, gl, x2d, mod3, ln_g, ln_b, wa, wb, wc, wo, seq, alpha, tm=512):
    t, d = x2d.shape
    per_b = seq // tm
    row = lambda w: pl.BlockSpec((tm, w), lambda i: (i, 0))
    whole = lambda a: pl.BlockSpec(a.shape, lambda i: (0, 0), pipeline_mode=pl.Buffered(1))
    modspec = lambda k: pl.BlockSpec((1, 1, d), lambda i: (i // per_b, 0, k))
    vec = pl.BlockSpec((1, d), lambda i: (0, 0))
    merged = pl.pallas_call(
        _branch_kernel,
        grid=(t // tm,),
        in_specs=[row(oa.shape[1]), row(ob.shape[1]), row(oc.shape[1]), row(gl.shape[1]),
                  whole(wa), whole(wb), whole(wc)],
        out_specs=row(d),
        out_shape=jax.ShapeDtypeStruct((t, d), BF16),
        compiler_params=_cparams("parallel"),
        name="branch_merge",
    )(oa, ob, oc, gl, wa, wb, wc)
    return pl.pallas_call(
        functools.partial(_outproj_kernel, alpha=alpha),
        grid=(t // tm,),
        in_specs=[row(d), row(d), modspec(2), modspec(4), modspec(3), vec, vec, whole(wo)],
        out_specs=[pl.BlockSpec((tm, d), lambda i: (i, 0)), pl.BlockSpec((d, tm), lambda i: (0, i))],
        out_shape=[jax.ShapeDtypeStruct((t, d), F32), jax.ShapeDtypeStruct((d, t), BF16)],
        compiler_params=_cparams("parallel"),
        name="outproj_ln",
    )(merged, x2d, mod3, mod3, mod3, ln_g.reshape(1, d), ln_b.reshape(1, d), wo)


CAND_ROWS = 4


def _top_exact(s, n, order):
    nrow = lax.broadcasted_iota(jnp.int32, (n, s.shape[1]), 0)

    def step(i, carry):
        s, rank, vals = carry
        m = jnp.max(s, axis=0, keepdims=True)
        first = jnp.min(jnp.where(s == m, order, jnp.iinfo(jnp.int32).max), axis=0, keepdims=True)
        hit = order == first
        return jnp.where(hit, -jnp.inf, s), jnp.where(hit, i, rank), jnp.where(nrow == i, m, vals)

    _, rank, vals = lax.fori_loop(0, n, step, (s, jnp.full(s.shape, n, jnp.int32), jnp.zeros(nrow.shape, F32)))
    return vals, rank


def _candidates(v1, v2):
    k, c = PEER_TOPK, CAND_ROWS
    ki = lax.broadcasted_iota(jnp.int32, v1.shape, 0)
    v1_tail = jnp.where(ki >= c, v1, -jnp.inf)
    cand = jnp.concatenate([v1[i:i + 1] + v2 for i in range(c)] + [v2[j:j + 1] + v1_tail for j in range(c)], axis=0)
    flat = jnp.concatenate([ki + i * k for i in range(c)] +
                           [jnp.where(ki >= c, ki * k + j, -1) for j in range(c)], axis=0)
    return cand, flat


def _route_outputs(s1, s2, v1, v2, rank2, sel, is_ith):
    k, c = PEER_TOPK, CAND_ROWS
    e1 = jnp.exp(v1 - v1[0:1])
    e2 = jnp.exp(v2 - v2[0:1])
    prod = jnp.concatenate([e1[i:i + 1] * e2 for i in range(c)] + [e2[j:j + 1] * e1 for j in range(c)], axis=0)
    z = jnp.sum(jnp.where(sel, prod, 0.0), axis=0, keepdims=True)
    self32 = sel.astype(F32)
    tail = functools.reduce(lambda a, b: a + b, [self32[(c + j) * k:(c + j + 1) * k] for j in range(c)])
    ja = jnp.zeros(s1.shape, F32)
    for i in range(k):
        cnt = jnp.sum(self32[i * k:(i + 1) * k], axis=0, keepdims=True) if i < c else tail[i:i + 1]
        ja = jnp.where(is_ith(i), cnt, ja)
    return (rank2.astype(F32).astype(BF16), jnp.exp(s2 - v2[0:1]).astype(BF16), ja,
            jnp.exp(s1 - v1[0:1]) / z)


def _route_exact(s1, s2):
    k = PEER_TOPK
    rows = lax.broadcasted_iota(jnp.int32, s1.shape, 0)
    v1, rank1 = _top_exact(s1, k, rows)
    v2, rank2 = _top_exact(s2, k, rows)
    cand, flat = _candidates(v1, v2)
    _, crank = _top_exact(cand, k, flat)
    return _route_outputs(s1, s2, v1, v2, rank2, crank < k, lambda i: rank1 == i)


def _sort_network(n):
    def merge(lo, hi, r):
        step = r * 2
        if step < hi - lo:
            yield from merge(lo, hi, step)
            yield from merge(lo + r, hi, step)
            yield from [(i, i + r) for i in range(lo + r, hi - r, step)]
        else:
            yield (lo, lo + r)

    def sort(lo, hi):
        if hi - lo >= 1:
            mid = lo + (hi - lo) // 2
            yield from sort(lo, mid)
            yield from sort(mid + 1, hi)
            yield from merge(lo, hi, 1)

    return tuple(sort(0, n - 1))


def _rep_max(x):
    return jnp.broadcast_to(jnp.max(x, axis=0, keepdims=True), x.shape)


def _rep_sum(x):
    return jnp.broadcast_to(jnp.sum(x, axis=0, keepdims=True), x.shape)


def _top_sorted(groups, n):
    work = [list(g) for g in groups]
    for i, j in _sort_network(len(work[0])):
        for v in work:
            v[i], v[j] = jnp.maximum(v[i], v[j]), jnp.minimum(v[i], v[j])
    vals = [[] for _ in work]
    for it in range(n):
        for v, out in zip(work, vals):
            m = _rep_max(v[0])
            out.append(m)
            hit = v[0] == m
            for k in range(n - 1 - it):
                v[k] = jnp.where(hit, v[k + 1], v[k])
    return vals


def _route_sorted(s1, s2):
    k, c, sub = PEER_TOPK, CAND_ROWS, SUBLANES
    nblk = s1.shape[0] // sub
    b1 = [s1[sub * q:sub * (q + 1)] for q in range(nblk)]
    b2 = [s2[sub * q:sub * (q + 1)] for q in range(nblk)]
    v1, v2 = _top_sorted([b1, b2], k)
    srow = lax.broadcasted_iota(jnp.int32, b1[0].shape, 0)

    def stack(vals):
        return [functools.reduce(lambda acc, t: jnp.where(srow == t, vals[blk * sub + t], acc),
                                 range(1, sub), vals[blk * sub]) for blk in range(k // sub)]

    st1, st2 = stack(v1), stack(v2)
    st1_tail = [jnp.where(srow >= c, st1[0], -jnp.inf)] + st1[1:]
    cand = ([v1[i] + st2[b] for i in range(c) for b in range(k // sub)] +
            [v2[j] + st1_tail[b] for j in range(c) for b in range(k // sub)])
    cv = _top_sorted([cand], k)[0]
    sel = [x >= cv[k - 1] for x in cand]
    e1s = [jnp.exp(x - v1[0]) for x in st1]
    e2s = [jnp.exp(x - v2[0]) for x in st2]
    prod = ([jnp.exp(v1[i] - v1[0]) * e2s[b] for i in range(c) for b in range(k // sub)] +
            [jnp.exp(v2[j] - v2[0]) * e1s[b] for j in range(c) for b in range(k // sub)])
    z = _rep_sum(functools.reduce(lambda a, b: a + b, [jnp.where(s, p, 0.0) for s, p in zip(sel, prod)]))
    ones = [jnp.where(s, 1.0, 0.0) for s in sel]
    per = k // sub
    tail = [functools.reduce(lambda a, b: a + b, [ones[(c + j) * per + b] for j in range(c)]) for b in range(per)]
    counts = []
    for i in range(k):
        if i < c:
            counts.append(_rep_sum(functools.reduce(lambda a, b: a + b, ones[i * per:(i + 1) * per])))
        else:
            counts.append(_rep_sum(jnp.where(srow == i % sub, tail[i // sub], 0.0)))
    inv_z = 1.0 / z
    ja, cf, r2, e2 = [], [], [], []
    for q in range(nblk):
        jq = jnp.zeros(b1[q].shape, F32)
        rq = jnp.zeros(b2[q].shape, F32)
        for i in range(k):
            jq = jnp.where(b1[q] == v1[i], counts[i], jq)
            rq = jnp.where(b2[q] < v2[i], float(i + 1), rq)
        ja.append(jq)
        r2.append(rq)
        cf.append(jnp.exp(b1[q] - v1[0]) * inv_z)
        e2.append(jnp.exp(b2[q] - v2[0]))

    def picked(blocks, thr):
        return _rep_sum(functools.reduce(lambda a, b: a + b, [jnp.where(x >= thr, 1.0, 0.0) for x in blocks]))

    def repeats(vals):
        return functools.reduce(lambda a, b: a | b, [vals[i] == vals[i + 1] for i in range(k - 1)])

    tie = ((picked(b1, v1[k - 1]) != k) | (picked(b2, v2[k - 1]) != k) | (picked(cand, cv[k - 1]) != k) |
           repeats(v1) | repeats(v2) | repeats(cv))
    cat = lambda blocks: jnp.concatenate(blocks, axis=0)
    return (cat(r2).astype(BF16), cat(e2).astype(BF16), cat(ja), cat(cf)), tie.astype(jnp.int32)


def _route_kernel(ht_ref, wqt_ref, keys_ref, r2_ref, e2_ref, ja_ref, cf_ref, qt_scr):
    nk = PEER_N_KEYS
    tm = ht_ref.shape[1]
    qt_scr[...] = jnp.dot(wqt_ref[...], ht_ref[...], preferred_element_type=F32).astype(BF16)

    def head(h, carry):
        q0 = pl.multiple_of(h * 2 * nk, 2 * nk)
        s1 = jnp.dot(keys_ref[0], qt_scr[pl.ds(q0, nk), :], preferred_element_type=F32)
        s2 = jnp.dot(keys_ref[1], qt_scr[pl.ds(q0 + nk, nk), :], preferred_element_type=F32)
        def store(cols, res):
            r2_ref[h, :, cols], e2_ref[h, :, cols], ja_ref[h, :, cols], cf_ref[h, :, cols] = res

        for g in range(tm // LANES):
            cols = slice(g * LANES, (g + 1) * LANES)
            res, tie = _route_sorted(s1[:, cols], s2[:, cols])
            store(cols, res)

            @pl.when(jnp.max(tie) > 0)
            def _(cols=cols):
                store(cols, _route_exact(s1[:, cols], s2[:, cols]))
        return carry

    lax.fori_loop(0, PEER_HEADS, head, 0)


def _peer_route(ht, wqt, keys_bf, tm=512):
    d, t = ht.shape
    shp = lambda dt: jax.ShapeDtypeStruct((PEER_HEADS, PEER_N_KEYS, t), dt)
    spec = pl.BlockSpec((PEER_HEADS, PEER_N_KEYS, tm), lambda i: (0, 0, i))
    return pl.pallas_call(
        _route_kernel,
        grid=(t // tm,),
        in_specs=[pl.BlockSpec((d, tm), lambda i: (0, i)),
                  pl.BlockSpec(wqt.shape, lambda i: (0, 0)),
                  pl.BlockSpec(keys_bf.shape, lambda i: (0, 0, 0))],
        out_specs=[spec] * 4,
        out_shape=[shp(BF16), shp(BF16), shp(F32), shp(F32)],
        scratch_shapes=[pltpu.VMEM((wqt.shape[0], tm), BF16)],
        compiler_params=_cparams("parallel"),
        name="peer_route",
    )(ht, wqt, keys_bf)


def _cast_kernel(x_ref, o_ref, *, transpose):
    x = x_ref[...]
    o_ref[...] = (x.T if transpose else x).astype(BF16)


def _layer_bf16(w_all, layer, transpose, tr=512):
    _, r, c = w_all.shape
    return pl.pallas_call(
        functools.partial(_cast_kernel, transpose=transpose),
        grid=(r // tr,),
        in_specs=[pl.BlockSpec((None, tr, c), lambda i: (layer, i, 0))],
        out_specs=pl.BlockSpec((c, tr), lambda i: (0, i)) if transpose else pl.BlockSpec((tr, c), lambda i: (i, 0)),
        out_shape=jax.ShapeDtypeStruct((c, r) if transpose else (r, c), BF16),
        compiler_params=_cparams("parallel"),
        name="transpose_cast" if transpose else "cast",
    )(w_all)


EXP_CHUNK = 512
GATE_LANES = 256

def _gated_hidden(gel, a0, r2_ref, e2_ref, ja_ref, cf_ref, w_ref):
    nk = PEER_N_KEYS
    na = gel.shape[0] // nk
    zero = jnp.zeros((), BF16)
    tm = gel.shape[1]

    def row(ref, h, aa):
        return ref[h, pl.ds(a0 + aa, 1), :].astype(BF16)

    ja_rows = [[row(ja_ref, h, aa) for aa in range(na)] for h in range(PEER_HEADS)]
    cf_rows = [[row(cf_ref, h, aa) for aa in range(na)] for h in range(PEER_HEADS)]
    for lg in range(tm // GATE_LANES):
        lanes = slice(lg * GATE_LANES, (lg + 1) * GATE_LANES)
        acc = [None] * na
        for h in range(PEER_HEADS):
            r2c, e2c = r2_ref[h, :, lanes], e2_ref[h, :, lanes]
            for aa in range(na):
                term = jnp.where(r2c < ja_rows[h][aa][:, lanes], e2c * cf_rows[h][aa][:, lanes], zero)
                acc[aa] = term if acc[aa] is None else acc[aa] + term
        for aa in range(na):
            rows = slice(aa * nk, (aa + 1) * nk)
            w_ref[rows, lanes] = acc[aa] * gel[rows, lanes]


def _experts_kernel(ht_ref, u_ref, vt_ref, r2_ref, e2_ref, ja_ref, cf_ref, x_ref, gf_ref, lg_ref, lb_ref,
                    o_ref, w_scr, yt_ref, *, alpha):
    j = pl.program_id(1)
    te = u_ref.shape[0]

    @pl.when(j == 0)
    def _():
        yt_ref[...] = jnp.zeros_like(yt_ref)

    ht = ht_ref[...]
    nchunk = te // EXP_CHUNK

    def hidden(c):
        return jnp.dot(u_ref[c * EXP_CHUNK:(c + 1) * EXP_CHUNK, :], ht, preferred_element_type=F32)

    act = hidden(0)
    for c in range(nchunk):
        nxt = hidden(c + 1) if c + 1 < nchunk else None
        gel = (0.5 * act * (1.0 + lax.erf(act * (2.0 ** -0.5)))).astype(BF16)
        a0 = j * (te // PEER_N_KEYS) + c * (EXP_CHUNK // PEER_N_KEYS)
        _gated_hidden(gel, a0, r2_ref, e2_ref, ja_ref, cf_ref, w_scr.at[c])
        yt_ref[...] += jnp.dot(vt_ref[:, c * EXP_CHUNK:(c + 1) * EXP_CHUNK], w_scr[c],
                               preferred_element_type=F32)
        act = nxt

    @pl.when(j == pl.num_programs(1) - 1)
    def _():
        o_ref[...] = _layer_norm(alpha * x_ref[...] + (1.0 + gf_ref[0]) * yt_ref[...].T, lg_ref[...], lb_ref[...])


def _peer_experts(ht, u_bf, vt_bf, r2, e2, ja, cf, x1, mod3, ln_g, ln_b, seq, alpha, tm=512, te=1024):
    d, t = ht.shape
    ne = u_bf.shape[0]
    per_b = seq // tm
    side = pl.BlockSpec((PEER_HEADS, PEER_N_KEYS, tm), lambda i, j: (0, 0, i))
    vec = pl.BlockSpec((1, d), lambda i, j: (0, 0))
    return pl.pallas_call(
        functools.partial(_experts_kernel, alpha=alpha),
        grid=(t // tm, ne // te),
        in_specs=[pl.BlockSpec((d, tm), lambda i, j: (0, i)),
                  pl.BlockSpec((te, d), lambda i, j: (j, 0)),
                  pl.BlockSpec((d, te), lambda i, j: (0, j)),
                  side, side, side, side,
                  pl.BlockSpec((tm, d), lambda i, j: (i, 0), pipeline_mode=pl.Buffered(1)),
                  pl.BlockSpec((1, 1, d), lambda i, j: (i // per_b, 0, 5)), vec, vec],
        out_specs=pl.BlockSpec((tm, d), lambda i, j: (i, 0)),
        out_shape=jax.ShapeDtypeStruct((t, d), F32),
        scratch_shapes=[pltpu.VMEM((te // EXP_CHUNK, EXP_CHUNK, tm), BF16), pltpu.VMEM((d, tm), F32)],
        compiler_params=_cparams("parallel", "arbitrary"),
        name="peer_experts",
    )(ht, u_bf, vt_bf, r2, e2, ja, cf, x1, mod3, ln_g.reshape(1, d), ln_b.reshape(1, d))


def kernel(x, c, w_mod, b_mod, w_in, a_q_gain, a_k_gain, c_sink, w_pa, w_pb, w_pc, w_o,
           ln1_g, ln1_b, peer_wq, peer_keys, peer_u, peer_v, ln2_g, ln2_b):
    nb, seq, d = x.shape
    depth = w_mod.shape[0]
    alpha = float((2 * depth) ** 0.25)
    ctab, stab = _rope_tables(seq)
    mod = _modulation(c, w_mod, b_mod)
    x2d = x.reshape(nb * seq, d)
    for l in range(depth):
        mod3 = mod[l].reshape(nb, 1, 6 * d)
        qkv_a, qkv_b, qkv_c, gl = _in_projection(x2d, mod3, w_in, l, seq)
        oa, u_bf, vt_bf = _mixer_a(qkv_a, ctab, stab, a_q_gain[l], a_k_gain[l], peer_u, peer_v, l, nb, seq)
        ob = _mixer_b(qkv_b, nb, seq)
        oc = _mixer_c(qkv_c, c_sink[l], nb, seq)
        x1, ht = _merge_project(oa, ob, oc, gl, x2d, mod3, ln1_g[l], ln1_b[l],
                                w_pa[l].astype(BF16), w_pb[l].astype(BF16), w_pc[l].astype(BF16),
                                w_o[l].astype(BF16), seq, alpha)
        r2, e2, ja, cf = _peer_route(ht, _layer_bf16(peer_wq, l, True), peer_keys[l].astype(BF16))
        x2d = _peer_experts(ht, u_bf, vt_bf, r2, e2, ja, cf, x1, mod3, ln2_g[l], ln2_b[l], seq, alpha)
    return x2d.reshape(nb, seq, d)
```
